```python
import math
import jax
import jax.numpy as jnp
from jax import lax
import numpy as np

D_MODEL = 1024
BATCH = 2
SEQ = 8192
DEPTH = 2
DEC_BATCH = 32
DEC_SEQ = 1
PAST_LEN = 8192
PAGE_SIZE = 128

N_A_LAYERS = DEPTH // 2
MIX_W = D_MODEL
MAIN_W = (3 * MIX_W) // 4
MEM_W = MIX_W - MAIN_W
SSM_GC = 16
SSM_G = MAIN_W // SSM_GC
SSM_P = 64
FOX_DH = 64
FOX_H = MAIN_W // FOX_DH
MEM_H = 4
MEM_DH = MEM_W // MEM_H
N_MEM = 256
Q_BLOCK = 128
MOE_GROUPS = 4
MOE_EPG = 4
MOE_E = MOE_GROUPS * MOE_EPG
MOE_TOPK = 2
MOE_F = D_MODEL // 4
DN_ALPHA = (2.0 * DEPTH) ** 0.25
DN_BETA = (8.0 * DEPTH) ** -0.25
LN_EPS = 1e-5
NEG_INF = -1e30

kernel_name = 'fox_s5_yoco_hmoe_decode_step'


def layer_norm(x, g, b):
    xf = x.astype(jnp.float32)
    mu = jnp.mean(xf, -1, keepdims=True)
    var = jnp.mean(jnp.square(xf - mu), -1, keepdims=True)
    y = (xf - mu) * lax.rsqrt(var + LN_EPS)
    return (y * g.astype(jnp.float32) + b.astype(jnp.float32)).astype(x.dtype)


def _complex_affine_combine(e1, e2):
    a1r, a1i, b1r, b1i = e1
    a2r, a2i, b2r, b2i = e2
    ar = a2r * a1r - a2i * a1i
    ai = a2r * a1i + a2i * a1r
    br = a2r * b1r - a2i * b1i + b2r
    bi = a2r * b1i + a2i * b1r + b2i
    return (ar, ai, br, bi)


def s5_mixer(u, h0_re, h0_im, a_re, a_im, log_dt, b_re, b_im, c_re, c_im, d_skip, w_glu, b_glu):
    f32 = jnp.float32
    bsz, t = u.shape[0], u.shape[1]
    ug = u.astype(f32).reshape(bsz, t, SSM_G, SSM_GC)
    a_re = a_re.astype(f32)
    a_im = a_im.astype(f32)
    dt = jnp.exp(log_dt.astype(f32))[:, None]
    mag = jnp.exp(a_re * dt)
    lb_re = mag * jnp.cos(a_im * dt)
    lb_im = mag * jnp.sin(a_im * dt)
    den = a_re * a_re + a_im * a_im
    zr = ((lb_re - 1.0) * a_re + lb_im * a_im) / den
    zi = (lb_im * a_re - (lb_re - 1.0) * a_im) / den
    b_re = b_re.astype(f32)
    b_im = b_im.astype(f32)
    bb_re = zr[..., None] * b_re - zi[..., None] * b_im
    bb_im = zr[..., None] * b_im + zi[..., None] * b_re
    bu_re = jnp.einsum('btgc,gpc->btgp', ug, bb_re)
    bu_im = jnp.einsum('btgc,gpc->btgp', ug, bb_im)
    a_r = jnp.broadcast_to(lb_re, bu_re.shape)
    a_i = jnp.broadcast_to(lb_im, bu_im.shape)
    _, _, h_re, h_im = lax.associative_scan(_complex_affine_combine, (a_r, a_i, bu_re, bu_im), axis=1)
    if h0_re is not None:
        steps = jnp.arange(1, t + 1, dtype=f32)[:, None, None]
        mag_k = jnp.exp(a_re * dt * steps)
        pr = mag_k * jnp.cos(a_im * dt * steps)
        pim = mag_k * jnp.sin(a_im * dt * steps)
        h0r = h0_re.astype(f32)[:, None]
        h0i = h0_im.astype(f32)[:, None]
        h_re = h_re + pr * h0r - pim * h0i
        h_im = h_im + pr * h0i + pim * h0r
    y = (jnp.einsum('btgp,gcp->btgc', h_re, c_re.astype(f32))
         - jnp.einsum('btgp,gcp->btgc', h_im, c_im.astype(f32))
         + d_skip.astype(f32).reshape(SSM_G, SSM_GC) * ug)
    y = y.reshape(bsz, t, MAIN_W)
    z = jax.nn.gelu(y)
    out = z * jax.nn.sigmoid(z @ w_glu.astype(f32) + b_glu.astype(f32))
    return out.astype(u.dtype), h_re[:, -1], h_im[:, -1]


def mem_kv(mem, w):
    bsz, n = mem.shape[0], mem.shape[1]
    kv = mem @ w
    k = kv[..., :MEM_W].reshape(bsz, n, MEM_H, MEM_DH)
    v = kv[..., MEM_W:].reshape(bsz, n, MEM_H, MEM_DH)
    return k, v


def mem_attend(qm, mk, mv):
    bsz, t = qm.shape[0], qm.shape[1]
    s = jnp.einsum('bthd,bmhd->bhtm', qm, mk).astype(jnp.float32) * (MEM_DH ** -0.5)
    p = jax.nn.softmax(s, axis=-1).astype(mv.dtype)
    return jnp.einsum('bhtm,bmhd->bthd', p, mv).reshape(bsz, t, MEM_W)


def shared_kv(h, w_kvf, b_f):
    bsz, t = h.shape[0], h.shape[1]
    kvf = h @ w_kvf
    k = kvf[..., :MAIN_W].reshape(bsz, t, FOX_H, FOX_DH)
    v = kvf[..., MAIN_W:2 * MAIN_W].reshape(bsz, t, FOX_H, FOX_DH)
    logf = jax.nn.log_sigmoid((kvf[..., 2 * MAIN_W:] + b_f).astype(jnp.float32)).astype(h.dtype)
    return k, v, logf


def fox_attend(q, c_q, pos_q, k, v, c_k, pos_k):
    s = jnp.einsum('bqhd,bkhd->bhqk', q, k).astype(jnp.float32) * (FOX_DH ** -0.5)
    bias = jnp.swapaxes(c_q, 1, 2)[:, :, :, None] - jnp.swapaxes(c_k, 1, 2)[:, :, None, :]
    s = jnp.where(pos_k[None, :] <= pos_q[:, None], s + bias, NEG_INF)
    p = jax.nn.softmax(s, axis=-1).astype(v.dtype)
    return jnp.einsum('bhqk,bkhd->bqhd', p, v)


def fox_prompt(q, k, v, logf):
    bsz, t = q.shape[0], q.shape[1]
    c = jnp.cumsum(logf.astype(jnp.float32), axis=1)
    nb = t // Q_BLOCK
    pos = jnp.arange(t, dtype=jnp.int32)
    qb = q.reshape(bsz, nb, Q_BLOCK, FOX_H, FOX_DH).transpose(1, 0, 2, 3, 4)
    cb = c.reshape(bsz, nb, Q_BLOCK, FOX_H).transpose(1, 0, 2, 3)
    pb = pos.reshape(nb, Q_BLOCK)
    out = lax.map(lambda blk: fox_attend(blk[0], blk[1], blk[2], k, v, c, pos), (qb, cb, pb))
    return out.transpose(1, 0, 2, 3, 4).reshape(bsz, t, MAIN_W)


def fox_decode(q, k_new, v_new, logf_new, cache_k, cache_v, cache_logf, page_table):
    bsz, n_pages = page_table.shape
    past = n_pages * PAGE_SIZE
    t = q.shape[1]
    k_past = cache_k[page_table].reshape(bsz, past, FOX_H, FOX_DH)
    v_past = cache_v[page_table].reshape(bsz, past, FOX_H, FOX_DH)
    lf_past = cache_logf[page_table].reshape(bsz, past, FOX_H)
    k = jnp.concatenate([k_past, k_new.astype(k_past.dtype)], axis=1)
    v = jnp.concatenate([v_past, v_new.astype(v_past.dtype)], axis=1)
    lf = jnp.concatenate([lf_past.astype(jnp.float32), logf_new.astype(jnp.float32)], axis=1)
    c = jnp.cumsum(lf, axis=1)
    pos_q = past + jnp.arange(t, dtype=jnp.int32)
    pos_k = jnp.arange(past + t, dtype=jnp.int32)
    return fox_attend(q, c[:, past:], pos_q, k, v, c, pos_k).reshape(bsz, t, MAIN_W)


def hier_moe(x, w_r1, b_r1, w_r2, b_r2, w_gate, w_up, w_down):
    f32 = jnp.float32
    shp = x.shape
    xt = x.reshape(-1, D_MODEL)
    g_logits = (xt @ w_r1 + b_r1).astype(f32)
    g_prob = jax.nn.softmax(g_logits, axis=-1)
    g_idx = jnp.argmax(g_logits, axis=-1)
    p_g = jnp.max(g_prob, axis=-1, keepdims=True)
    e_logits = (jnp.einsum('nd,gde->nge', xt, w_r2) + b_r2).astype(f32)
    e_sel = jnp.einsum('nge,ng->ne', e_logits, jax.nn.one_hot(g_idx, MOE_GROUPS, dtype=f32))
    top_v, top_i = lax.top_k(e_sel, MOE_TOPK)
    w_top = jax.nn.softmax(top_v, axis=-1) * p_g
    e_idx = g_idx[:, None] * MOE_EPG + top_i
    combine = jnp.sum(jax.nn.one_hot(e_idx, MOE_E, dtype=f32) * w_top[..., None], axis=1)
    h = jax.nn.silu(jnp.einsum('nd,edf->nef', xt, w_gate)) * jnp.einsum('nd,edf->nef', xt, w_up)
    out = jnp.einsum('nef,efd->nd', h * combine[..., None].astype(h.dtype), w_down)
    return out.reshape(shp).astype(x.dtype)


def setup_inputs(seed: int = 0) -> dict:
    key = jax.random.key(seed)
    kk = jax.random.split(key, 40)
    f32 = jnp.float32

    def nrm(k, shape, scale=1.0):
        return jax.random.normal(k, shape, f32) * scale

    n_pages = PAST_LEN // PAGE_SIZE
    n_used = DEC_BATCH * n_pages
    n_pool = n_used + max(1, n_used // 4)
    page_table = jax.random.permutation(kk[9], n_pool)[:n_used].reshape(DEC_BATCH, n_pages).astype(jnp.int32)
    a_im0 = math.pi * jnp.arange(SSM_P, dtype=f32)
    return {
        'x_prompt': nrm(kk[0], (BATCH, SEQ, D_MODEL)),
        'x_sample': nrm(kk[1], (DEC_BATCH, DEC_SEQ, D_MODEL)),
        'cache_k': nrm(kk[2], (n_pool, PAGE_SIZE, FOX_H, FOX_DH)),
        'cache_v': nrm(kk[3], (n_pool, PAGE_SIZE, FOX_H, FOX_DH)),
        'cache_logf': jax.nn.log_sigmoid(2.5 + nrm(kk[4], (n_pool, PAGE_SIZE, FOX_H), 0.5)),
        'cache_mem_k': nrm(kk[5], (DEPTH, DEC_BATCH, N_MEM, MEM_H, MEM_DH)),
        'cache_mem_v': nrm(kk[6], (DEPTH, DEC_BATCH, N_MEM, MEM_H, MEM_DH)),
        'state_ssm_re': nrm(kk[7], (N_A_LAYERS, DEC_BATCH, SSM_G, SSM_P), 0.2),
        'state_ssm_im': nrm(kk[8], (N_A_LAYERS, DEC_BATCH, SSM_G, SSM_P), 0.2),
        'page_table': page_table,
        'mem_prompt': nrm(kk[10], (BATCH, N_MEM, D_MODEL)),
        'w_in': nrm(kk[11], (DEPTH, D_MODEL, MIX_W), D_MODEL ** -0.5),
        'w_out': nrm(kk[12], (DEPTH, MIX_W, D_MODEL), DN_BETA * MIX_W ** -0.5),
        'ln1_g': 1.0 + nrm(kk[13], (DEPTH, D_MODEL), 0.01),
        'ln1_b': nrm(kk[14], (DEPTH, D_MODEL), 0.01),
        'ln2_g': 1.0 + nrm(kk[15], (DEPTH, D_MODEL), 0.01),
        'ln2_b': nrm(kk[16], (DEPTH, D_MODEL), 0.01),
        'w_mem_kv': nrm(kk[17], (DEPTH, D_MODEL, 2 * MEM_W), D_MODEL ** -0.5),
        'ssm_a_re': -0.5 + nrm(kk[18], (N_A_LAYERS, SSM_G, SSM_P), 0.01),
        'ssm_a_im': a_im0 + nrm(kk[19], (N_A_LAYERS, SSM_G, SSM_P), 0.01),
        'ssm_log_dt': jax.random.uniform(kk[20], (N_A_LAYERS, SSM_G), f32, minval=math.log(1e-3), maxval=math.log(1e-1)),
        'ssm_b_re': nrm(kk[21], (N_A_LAYERS, SSM_G, SSM_P, SSM_GC), (2.0 * SSM_GC) ** -0.5),
        'ssm_b_im': nrm(kk[22], (N_A_LAYERS, SSM_G, SSM_P, SSM_GC), (2.0 * SSM_GC) ** -0.5),
        'ssm_c_re': nrm(kk[23], (N_A_LAYERS, SSM_G, SSM_GC, SSM_P), SSM_P ** -0.5),
        'ssm_c_im': nrm(kk[24], (N_A_LAYERS, SSM_G, SSM_GC, SSM_P), SSM_P ** -0.5),
        'ssm_d': nrm(kk[25], (N_A_LAYERS, MAIN_W)),
        'w_glu': nrm(kk[26], (N_A_LAYERS, MAIN_W, MAIN_W), MAIN_W ** -0.5),
        'b_glu': nrm(kk[27], (N_A_LAYERS, MAIN_W), 0.01),
        'w_kvf': nrm(kk[28], (D_MODEL, 2 * MAIN_W + FOX_H), D_MODEL ** -0.5),
        'b_f': 2.5 + nrm(kk[29], (FOX_H,), 0.1),
        'w_r1': nrm(kk[30], (DEPTH, D_MODEL, MOE_GROUPS), D_MODEL ** -0.5),
        'b_r1': nrm(kk[31], (DEPTH, MOE_GROUPS), 0.01),
        'w_r2': nrm(kk[32], (DEPTH, MOE_GROUPS, D_MODEL, MOE_EPG), D_MODEL ** -0.5),
        'b_r2': nrm(kk[33], (DEPTH, MOE_GROUPS, MOE_EPG), 0.01),
        'w_e_gate': nrm(kk[34], (DEPTH, MOE_E, D_MODEL, MOE_F), D_MODEL ** -0.5),
        'w_e_up': nrm(kk[35], (DEPTH, MOE_E, D_MODEL, MOE_F), D_MODEL ** -0.5),
        'w_e_down': nrm(kk[36], (DEPTH, MOE_E, MOE_F, D_MODEL), DN_BETA * MOE_F ** -0.5),
    }


def reference(x_prompt, x_sample, cache_k, cache_v, cache_logf, cache_mem_k, cache_mem_v,
              state_ssm_re, state_ssm_im, page_table, mem_prompt, w_in, w_out, ln1_g, ln1_b,
              ln2_g, ln2_b, w_mem_kv, ssm_a_re, ssm_a_im, ssm_log_dt, ssm_b_re, ssm_b_im,
              ssm_c_re, ssm_c_im, ssm_d, w_glu, b_glu, w_kvf, b_f, w_r1, b_r1, w_r2, b_r2,
              w_e_gate, w_e_up, w_e_down):

    def trunk(x, mem_k, mem_v, h0_re, h0_im, attend_shared):
        bsz, t = x.shape[0], x.shape[1]
        ssm_re, ssm_im = [], []
        shared = None
        for l in range(DEPTH):
            proj = x @ w_in[l]
            main = proj[..., :MAIN_W]
            qm = proj[..., MAIN_W:].reshape(bsz, t, MEM_H, MEM_DH)
            if l < N_A_LAYERS:
                main_out, hr, hi = s5_mixer(
                    main,
                    None if h0_re is None else h0_re[l],
                    None if h0_im is None else h0_im[l],
                    ssm_a_re[l], ssm_a_im[l], ssm_log_dt[l], ssm_b_re[l], ssm_b_im[l],
                    ssm_c_re[l], ssm_c_im[l], ssm_d[l], w_glu[l], b_glu[l])
                ssm_re.append(hr)
                ssm_im.append(hi)
            else:
                q = main.reshape(bsz, t, FOX_H, FOX_DH)
                main_out = attend_shared(q, shared[0], shared[1], shared[2])
            mem_out = mem_attend(qm, mem_k[l], mem_v[l])
            mixed = jnp.concatenate([main_out.astype(x.dtype), mem_out.astype(x.dtype)], axis=-1) @ w_out[l]
            x = layer_norm(DN_ALPHA * x + mixed, ln1_g[l], ln1_b[l])
            x = layer_norm(DN_ALPHA * x + hier_moe(x, w_r1[l], b_r1[l], w_r2[l], b_r2[l],
                                                   w_e_gate[l], w_e_up[l], w_e_down[l]),
                           ln2_g[l], ln2_b[l])
            if l == N_A_LAYERS - 1:
                shared = shared_kv(x, w_kvf, b_f)
        return x, jnp.stack(ssm_re), jnp.stack(ssm_im), shared

    p_mem_k = jnp.stack([mem_kv(mem_prompt, w_mem_kv[l])[0] for l in range(DEPTH)])
    p_mem_v = jnp.stack([mem_kv(mem_prompt, w_mem_kv[l])[1] for l in range(DEPTH)])
    y_prompt, p_ssm_re, p_ssm_im, p_shared = trunk(x_prompt, p_mem_k, p_mem_v, None, None, fox_prompt)
    p_k, p_v, p_logf = p_shared

    def attend_sample(q, k_new, v_new, logf_new):
        return fox_decode(q, k_new, v_new, logf_new, cache_k, cache_v, cache_logf, page_table)

    y_sample, s_ssm_re, s_ssm_im, s_shared = trunk(x_sample, cache_mem_k, cache_mem_v,
                                                   state_ssm_re, state_ssm_im, attend_sample)
    s_k, s_v, s_logf = s_shared

    return (y_prompt, y_sample, p_ssm_re, p_ssm_im, p_k, p_v, p_logf, p_mem_k, p_mem_v,
            s_ssm_re, s_ssm_im, s_k, s_v, s_logf)
```

```python
import functools
import math

import jax
import jax.numpy as jnp
from jax import lax
from jax.experimental import pallas as pl
from jax.experimental.pallas import tpu as pltpu

F32 = jnp.float32
BF16 = jnp.bfloat16

D_MODEL = 1024
DEPTH = 2
MAIN_W = 768
MEM_W = 256
SSM_GC = 16
SSM_G = 48
SSM_P = 64
FOX_DH = 64
FOX_H = 12
MEM_H = 4
MEM_DH = 64
PAGE_SIZE = 128
MOE_GROUPS = 4
MOE_EPG = 4
MOE_F = 256
DN_ALPHA = (2.0 * DEPTH) ** 0.25
LN_EPS = 1e-5
NEG_INF = -1e30

LANES = 128
S5_CHUNK = 8
S5_GPL = LANES // SSM_GC
S5_NLT = MAIN_W // LANES
S5_STATE = S5_GPL * SSM_P
VMEM_LIMIT = 56 * 1024 * 1024


def _cparams(sem):
    return pltpu.CompilerParams(dimension_semantics=sem, vmem_limit_bytes=VMEM_LIMIT)


def _proj_body(x_ref, w_ref, *o_refs, widths):
    y = jnp.dot(x_ref[...].astype(BF16), w_ref[...], preferred_element_type=F32)
    off = 0
    for o_ref, n in zip(o_refs, widths):
        o_ref[...] = y[:, off:off + n].astype(o_ref.dtype)
        off += n


def _proj(x, w_bf, widths, dtypes, tm):
    m, k = x.shape
    n = w_bf.shape[1]
    assert m % tm == 0 and sum(widths) == n
    return pl.pallas_call(
        functools.partial(_proj_body, widths=tuple(widths)),
        grid=(m // tm,),
        in_specs=[pl.BlockSpec((tm, k), lambda i: (i, 0)),
                  pl.BlockSpec((k, n), lambda i: (0, 0))],
        out_specs=[pl.BlockSpec((tm, w), lambda i: (i, 0)) for w in widths],
        out_shape=[jax.ShapeDtypeStruct((m, w), dt) for w, dt in zip(widths, dtypes)],
        compiler_params=_cparams(("arbitrary",)),
        name="proj",
    )(x, w_bf)


def _gelu(x):
    c = math.sqrt(2.0 / math.pi)
    return 0.5 * x * (1.0 + jnp.tanh(c * (x + 0.044715 * (x * x * x))))


def _s5_weights(a_re, a_im, log_dt, b_re, b_im, c_re, c_im):
    f32 = F32
    L = S5_CHUNK
    a_re, a_im = a_re.astype(f32), a_im.astype(f32)
    dt = jnp.exp(log_dt.astype(f32))[:, None]
    mag = jnp.exp(a_re * dt)
    lb_re = mag * jnp.cos(a_im * dt)
    lb_im = mag * jnp.sin(a_im * dt)
    den = a_re * a_re + a_im * a_im
    zr = ((lb_re - 1.0) * a_re + lb_im * a_im) / den
    zi = (lb_im * a_re - (lb_re - 1.0) * a_im) / den
    b_re, b_im = b_re.astype(f32), b_im.astype(f32)
    bb_re = zr[..., None] * b_re - zi[..., None] * b_im
    bb_im = zr[..., None] * b_im + zi[..., None] * b_re
    ks = jnp.arange(L + 1, dtype=f32)[:, None, None]
    mag_k = jnp.exp(a_re * dt * ks)
    pk_re = mag_k * jnp.cos(a_im * dt * ks)
    pk_im = mag_k * jnp.sin(a_im * dt * ks)
    c_re, c_im = c_re.astype(f32), c_im.astype(f32)

    eye = jnp.eye(S5_GPL, dtype=f32)

    pb_re = pk_re[..., None] * bb_re - pk_im[..., None] * bb_im
    pb_im = pk_re[..., None] * bb_im + pk_im[..., None] * bb_re

    def lt_w1(pb):
        w = pb[L - 1 - jnp.arange(L)]
        w = w.transpose(1, 0, 3, 2).reshape(S5_NLT, S5_GPL, L, SSM_GC, SSM_P)
        w = w.transpose(0, 2, 1, 3, 4)[:, :, :, :, None, :] * eye[None, None, :, None, :, None]
        return w.reshape(S5_NLT, L * LANES, S5_STATE)

    w1 = jnp.concatenate([lt_w1(pb_re), lt_w1(pb_im)], axis=-1)

    kk = (jnp.sum(pb_re[:L].transpose(0, 1, 3, 2)[:, :, :, None, :] * c_re[None, :, None, :, :], axis=-1)
          - jnp.sum(pb_im[:L].transpose(0, 1, 3, 2)[:, :, :, None, :] * c_im[None, :, None, :, :], axis=-1))
    ti = jnp.arange(L)[:, None]
    to = jnp.arange(L)[None, :]
    lag = jnp.clip(to - ti, 0, L - 1)
    mm = jnp.where((to >= ti)[:, :, None, None, None], kk[lag], 0.0)
    mm = mm.transpose(2, 0, 3, 1, 4).reshape(S5_NLT, S5_GPL, L, SSM_GC, L, SSM_GC)
    mm = mm.transpose(0, 2, 1, 3, 4, 5)[:, :, :, :, :, None, :] * eye[None, None, :, None, None, :, None]
    mm = mm.reshape(S5_NLT, L * LANES, L * LANES)

    def lt_w2(x):
        w = x.transpose(1, 3, 0, 2).reshape(S5_NLT, S5_GPL, SSM_P, L, SSM_GC)
        w = w[:, :, :, :, None, :] * eye[None, :, None, None, :, None]
        return w.reshape(S5_NLT, S5_STATE, L * LANES)

    ca_re = c_re[None] * pk_re[1:, :, None, :] - c_im[None] * pk_im[1:, :, None, :]
    ca_im = c_re[None] * pk_im[1:, :, None, :] + c_im[None] * pk_re[1:, :, None, :]
    w2 = jnp.concatenate([lt_w2(ca_re), lt_w2(-ca_im)], axis=1)

    def lt_vec(x):
        return x.reshape(S5_NLT, 1, S5_STATE)

    a_chunk = jnp.concatenate([lt_vec(pk_re[L]), lt_vec(pk_im[L])], axis=-1)
    a_one = jnp.concatenate([lt_vec(pk_re[1]), lt_vec(pk_im[1])], axis=-1)

    def lt_c0(x):
        w = x.transpose(0, 2, 1).reshape(S5_NLT, S5_GPL, SSM_P, SSM_GC)
        w = w[:, :, :, None, :] * eye[None, :, None, :, None]
        return w.reshape(S5_NLT, S5_STATE, LANES)

    c0 = jnp.concatenate([lt_c0(c_re), lt_c0(-c_im)], axis=1)
    bb1 = w1[:, (L - 1) * LANES:, :]
    return dict(w1m=jnp.concatenate([w1, mm], axis=-1).astype(BF16), w2=w2.astype(BF16),
                a_chunk=a_chunk, a_one=a_one, c0=c0.astype(BF16), bb1=bb1.astype(BF16))


def _s5_body(u_ref, w1m_ref, w2_ref, ach_ref, d_ref, z_ref, hfin_ref, s_scr, hp_scr, h_scr, *, tm):
    t = pl.program_id(2)
    nc = tm // S5_CHUNK
    ns = S5_STATE

    @pl.when(t == 0)
    def _():
        h_scr[...] = jnp.zeros_like(h_scr)

    parts = [u_ref[0, pl.ds(k, nc, stride=S5_CHUNK), :] for k in range(S5_CHUNK)]
    a = jnp.concatenate(parts, axis=1).astype(BF16)
    sy = jnp.dot(a, w1m_ref[0], preferred_element_type=F32)
    s_scr[...] = sy[:, :2 * ns]
    ar = ach_ref[0, :, :ns]
    ai = ach_ref[0, :, ns:]

    def step(j, carry):
        hr, hi = carry
        hp_scr[pl.ds(j, 1), :ns] = hr
        hp_scr[pl.ds(j, 1), ns:] = hi
        sr = s_scr[pl.ds(j, 1), :ns]
        si = s_scr[pl.ds(j, 1), ns:]
        return ar * hr - ai * hi + sr, ar * hi + ai * hr + si

    hr, hi = lax.fori_loop(0, nc, step, (h_scr[:, :ns], h_scr[:, ns:]), unroll=8)
    h_scr[:, :ns] = hr
    h_scr[:, ns:] = hi
    hfin_ref[0, 0] = h_scr[...]

    yi = jnp.dot(hp_scr[...].astype(BF16), w2_ref[0], preferred_element_type=F32)
    d = d_ref[...]
    for k in range(S5_CHUNK):
        yk = sy[:, 2 * ns + k * LANES:2 * ns + (k + 1) * LANES] + yi[:, k * LANES:(k + 1) * LANES] + d * parts[k]
        z_ref[0, pl.ds(k, nc, stride=S5_CHUNK), :] = _gelu(yk)


def _s5_prompt(u, sw, d_skip, tm):
    b, t, _ = u.shape
    assert t % tm == 0 and tm % (8 * S5_CHUNK) == 0
    nc = tm // S5_CHUNK
    ns2 = 2 * S5_STATE
    kw = S5_CHUNK * LANES
    return pl.pallas_call(
        functools.partial(_s5_body, tm=tm),
        grid=(S5_NLT, b, t // tm),
        in_specs=[pl.BlockSpec((1, tm, LANES), lambda l, bb, tt: (bb, tt, l)),
                  pl.BlockSpec((1, kw, ns2 + kw), lambda l, bb, tt: (l, 0, 0)),
                  pl.BlockSpec((1, ns2, kw), lambda l, bb, tt: (l, 0, 0)),
                  pl.BlockSpec((1, 1, ns2), lambda l, bb, tt: (l, 0, 0)),
                  pl.BlockSpec((1, LANES), lambda l, bb, tt: (0, l))],
        out_specs=[pl.BlockSpec((1, tm, LANES), lambda l, bb, tt: (bb, tt, l)),
                   pl.BlockSpec((1, 1, 1, ns2), lambda l, bb, tt: (bb, l, 0, 0))],
        out_shape=[jax.ShapeDtypeStruct(u.shape, F32),
                   jax.ShapeDtypeStruct((b, S5_NLT, 1, ns2), F32)],
        scratch_shapes=[pltpu.VMEM((nc, ns2), F32), pltpu.VMEM((nc, ns2), F32), pltpu.VMEM((1, ns2), F32)],
        compiler_params=_cparams(("arbitrary", "arbitrary", "arbitrary")),
        name="s5_prompt",
    )(u, sw["w1m"], sw["w2"], sw["a_chunk"], d_skip)


def _s5_step_body(u_ref, h0_ref, bb1_ref, c0_ref, a1_ref, d_ref, z_ref, h_ref):
    ns = S5_STATE
    for l in range(S5_NLT):
        u = u_ref[:, l * LANES:(l + 1) * LANES]
        bu = jnp.dot(u.astype(BF16), bb1_ref[l], preferred_element_type=F32)
        ar = a1_ref[l, :, :ns]
        ai = a1_ref[l, :, ns:]
        h0r = h0_ref[l, :, :ns]
        h0i = h0_ref[l, :, ns:]
        hr = ar * h0r - ai * h0i + bu[:, :ns]
        hi = ar * h0i + ai * h0r + bu[:, ns:]
        h_ref[l, :, :ns] = hr
        h_ref[l, :, ns:] = hi
        h = jnp.concatenate([hr, hi], axis=1).astype(BF16)
        y = jnp.dot(h, c0_ref[l], preferred_element_type=F32) + d_ref[:, l * LANES:(l + 1) * LANES] * u
        z_ref[:, l * LANES:(l + 1) * LANES] = _gelu(y)


def _s5_step(u, h0, sw, d_skip):
    n = u.shape[0]
    return pl.pallas_call(
        _s5_step_body,
        out_shape=[jax.ShapeDtypeStruct((n, MAIN_W), F32),
                   jax.ShapeDtypeStruct((S5_NLT, n, 2 * S5_STATE), F32)],
        compiler_params=pltpu.CompilerParams(vmem_limit_bytes=VMEM_LIMIT),
        name="s5_step",
    )(u, h0, sw["bb1"], sw["c0"], sw["a_one"], d_skip)


def _mem_attn_body(q_ref, k_ref, v_ref, o_ref):
    q = q_ref[0]
    tm = q.shape[0]
    rows = max(tm, 8)
    outs = []
    for h in range(MEM_H):
        sl = slice(h * MEM_DH, (h + 1) * MEM_DH)
        qh = q[:, sl]
        if tm < rows:
            qh = jnp.broadcast_to(qh, (rows, MEM_DH))
        s = lax.dot_general(qh, k_ref[0, :, sl], (((1,), (1,)), ((), ())),
                            preferred_element_type=F32) * (MEM_DH ** -0.5)
        m = jnp.max(s, axis=1, keepdims=True)
        e = jnp.exp(s - m)
        p = e / jnp.sum(e, axis=1, keepdims=True)
        o = jnp.dot(p.astype(BF16), v_ref[0, :, sl], preferred_element_type=F32)
        outs.append(o[:tm])
    o_ref[0] = jnp.concatenate(outs, axis=1).astype(o_ref.dtype)


def _mem_attn(q, k, v, tm):
    b, t, w = q.shape
    nm = k.shape[1]
    return pl.pallas_call(
        _mem_attn_body,
        grid=(b, t // tm),
        in_specs=[pl.BlockSpec((1, tm, w), lambda i, j: (i, j, 0)),
                  pl.BlockSpec((1, nm, w), lambda i, j: (i, 0, 0)),
                  pl.BlockSpec((1, nm, w), lambda i, j: (i, 0, 0))],
        out_specs=pl.BlockSpec((1, tm, w), lambda i, j: (i, j, 0)),
        out_shape=jax.ShapeDtypeStruct((b, t, w), BF16),
        compiler_params=_cparams(("arbitrary", "arbitrary")),
        name="mem_attn",
    )(q, k, v)


def _layer_norm(x, g, b):
    mu = jnp.mean(x, axis=-1, keepdims=True)
    xc = x - mu
    var = jnp.mean(xc * xc, axis=-1, keepdims=True)
    return xc * lax.rsqrt(var + LN_EPS) * g + b


def _route(logits):
    lane = lax.broadcasted_iota(jnp.int32, logits.shape, 1).astype(F32)
    lo = jnp.float32(jnp.finfo(jnp.float32).min)
    none = jnp.float32(LANES)
    is_g = lane < MOE_GROUPS
    gl = jnp.where(is_g, logits, lo)
    gm = jnp.max(gl, axis=1, keepdims=True)
    gsum = jnp.sum(jnp.where(is_g, jnp.exp(gl - gm), 0.0), axis=1, keepdims=True)
    p_g = 1.0 / gsum
    g_idx = jnp.min(jnp.where(is_g & (gl == gm), lane, none), axis=1, keepdims=True)
    e_lo = MOE_GROUPS + MOE_EPG * g_idx
    is_e = (lane >= e_lo) & (lane < e_lo + MOE_EPG)
    el = jnp.where(is_e, logits, lo)
    v1 = jnp.max(el, axis=1, keepdims=True)
    i1 = jnp.min(jnp.where(is_e & (el == v1), lane, none), axis=1, keepdims=True)
    is_e2 = is_e & (lane != i1)
    el2 = jnp.where(is_e2, logits, lo)
    v2 = jnp.max(el2, axis=1, keepdims=True)
    i2 = jnp.min(jnp.where(is_e2 & (el2 == v2), lane, none), axis=1, keepdims=True)
    r = jnp.exp(v2 - v1)
    w1 = p_g / (1.0 + r)
    w2 = p_g * r / (1.0 + r)
    out = jnp.where(lane == i1, w1, 0.0) + jnp.where(lane == i2, w2, 0.0)
    return out + jnp.where(lane == 0.0, g_idx, 0.0)


def _mix_body(*refs, glu):
    if glu:
        (x_ref, main_ref, mem_ref, wglu_ref, bglu_ref, wout_ref, g_ref, b_ref, wr_ref, br_ref,
         x1_ref, route_ref) = refs
        z = main_ref[...]
        gate = jnp.dot(z.astype(BF16), wglu_ref[...], preferred_element_type=F32) + bglu_ref[...]
        main = (z * jax.nn.sigmoid(gate)).astype(BF16)
    else:
        x_ref, main_ref, mem_ref, wout_ref, g_ref, b_ref, wr_ref, br_ref, x1_ref, route_ref = refs
        main = main_ref[...].astype(BF16)
    mixed = (jnp.dot(main, wout_ref[:MAIN_W, :], preferred_element_type=F32)
             + jnp.dot(mem_ref[...], wout_ref[MAIN_W:, :], preferred_element_type=F32))
    x1 = _layer_norm(DN_ALPHA * x_ref[...] + mixed, g_ref[...], b_ref[...])
    x1_ref[...] = x1
    logits = jnp.dot(x1.astype(BF16), wr_ref[...], preferred_element_type=F32) + br_ref[...]
    route_ref[...] = _route(logits)


def _mix(x, main, mem, w_out_bf, ln_g, ln_b, wr_bf, br, tm, w_glu_bf=None, b_glu=None):
    n = x.shape[0]
    glu = w_glu_bf is not None
    row = lambda w: pl.BlockSpec((tm, w), lambda i: (i, 0))
    full = lambda a: pl.BlockSpec(a.shape, lambda i: (0,) * a.ndim)
    args = [x, main, mem]
    specs = [row(D_MODEL), row(MAIN_W), row(MEM_W)]
    if glu:
        args += [w_glu_bf, b_glu]
        specs += [full(w_glu_bf), full(b_glu)]
    tail = [w_out_bf, ln_g, ln_b, wr_bf, br]
    args += tail
    specs += [full(a) for a in tail]
    return pl.pallas_call(
        functools.partial(_mix_body, glu=glu),
        grid=(n // tm,),
        in_specs=specs,
        out_specs=[row(D_MODEL), row(LANES)],
        out_shape=[jax.ShapeDtypeStruct((n, D_MODEL), F32), jax.ShapeDtypeStruct((n, LANES), F32)],
        compiler_params=_cparams(("arbitrary",)),
        name="mix_glu" if glu else "mix",
    )(*args)


def _moe_body(tg_ref, x_ref, cw_ref, wgu_ref, wd_ref, g_ref, b_ref, o_ref):
    del tg_ref
    x = x_ref[...]
    hf = MOE_EPG * MOE_F
    gu = jnp.dot(x.astype(BF16), wgu_ref[0], preferred_element_type=F32)
    h = jax.nn.silu(gu[:, :hf]) * gu[:, hf:]
    cw = cw_ref[...]
    hs = [h[:, e * MOE_F:(e + 1) * MOE_F] * cw[:, e:e + 1] for e in range(MOE_EPG)]
    hb = jnp.concatenate(hs, axis=1).astype(BF16)
    y = jnp.dot(hb, wd_ref[0], preferred_element_type=F32)
    o_ref[...] = _layer_norm(DN_ALPHA * x + y, g_ref[...], b_ref[...])


def _moe(tile_group, xs, cws, wgu_bf, wd_bf, ln_g, ln_b, tm):
    p = xs.shape[0]
    hf = MOE_EPG * MOE_F
    grid_spec = pltpu.PrefetchScalarGridSpec(
        num_scalar_prefetch=1,
        grid=(p // tm,),
        in_specs=[pl.BlockSpec((tm, D_MODEL), lambda i, tg: (i, 0)),
                  pl.BlockSpec((tm, MOE_EPG), lambda i, tg: (i, 0)),
                  pl.BlockSpec((1, D_MODEL, 2 * hf), lambda i, tg: (tg[i], 0, 0)),
                  pl.BlockSpec((1, hf, D_MODEL), lambda i, tg: (tg[i], 0, 0)),
                  pl.BlockSpec((1, D_MODEL), lambda i, tg: (0, 0)),
                  pl.BlockSpec((1, D_MODEL), lambda i, tg: (0, 0))],
        out_specs=pl.BlockSpec((tm, D_MODEL), lambda i, tg: (i, 0)),
    )
    return pl.pallas_call(
        _moe_body,
        grid_spec=grid_spec,
        out_shape=jax.ShapeDtypeStruct((p, D_MODEL), F32),
        compiler_params=_cparams(("arbitrary",)),
        name="moe",
    )(tile_group, xs, cws, wgu_bf, wd_bf, ln_g, ln_b)


MOE_TM = 256


def _moe_layer(x1_parts, route_parts, wgu_bf, wd_bf, ln_g, ln_b):
    x1 = jnp.concatenate(x1_parts, axis=0)
    route = jnp.concatenate(route_parts, axis=0)
    n = x1.shape[0]
    tm = MOE_TM
    g_idx = route[:, 0].astype(jnp.int32)
    cw16 = route[:, MOE_GROUPS:MOE_GROUPS + MOE_GROUPS * MOE_EPG].reshape(n, MOE_GROUPS, MOE_EPG)
    cw = jnp.take_along_axis(cw16, g_idx[:, None, None], axis=1)[:, 0, :]
    onehot = (g_idx[:, None] == jnp.arange(MOE_GROUPS)[None, :]).astype(jnp.int32)
    counts = jnp.sum(onehot, axis=0)
    rank = jnp.sum((jnp.cumsum(onehot, axis=0) - onehot) * onehot, axis=1)
    tiles = (counts + tm - 1) // tm
    tile_start = jnp.cumsum(tiles) - tiles
    pos = tile_start[g_idx] * tm + rank
    n_tiles = (n + tm - 1) // tm + MOE_GROUPS
    p = n_tiles * tm
    src = jnp.zeros((p,), jnp.int32).at[pos].set(jnp.arange(n, dtype=jnp.int32))
    valid = jnp.zeros((p,), F32).at[pos].set(1.0)
    tile_group = jnp.clip(jnp.searchsorted(jnp.cumsum(tiles), jnp.arange(n_tiles), side="right"),
                          0, MOE_GROUPS - 1).astype(jnp.int32)
    xs = x1[src]
    cws = cw[src] * valid[:, None]
    ys = _moe(tile_group, xs, cws, wgu_bf, wd_bf, ln_g, ln_b, tm)
    y = ys[pos]
    outs, off = [], 0
    for part in x1_parts:
        outs.append(y[off:off + part.shape[0]])
        off += part.shape[0]
    return outs


def _log_sigmoid(x):
    return -(jnp.maximum(-x, 0.0) + jnp.log1p(jnp.exp(-jnp.abs(x))))


def _split3(x):
    a1 = x.astype(BF16)
    r1 = x - a1.astype(F32)
    a2 = r1.astype(BF16)
    a3 = (r1 - a2.astype(F32)).astype(BF16)
    return a1, a2, a3


def _kvf_body(x_ref, wkv_ref, wf_ref, bf_ref, k_ref, v_ref, kb_ref, vb_ref, lf_ref, c_ref, carry):
    t = pl.program_id(1)

    @pl.when(t == 0)
    def _():
        carry[...] = jnp.zeros_like(carry)

    xb = x_ref[...].astype(BF16)
    tm = xb.shape[0]
    kv = jnp.dot(xb, wkv_ref[...], preferred_element_type=F32)
    k = kv[:, :MAIN_W]
    v = kv[:, MAIN_W:]
    k_ref[...] = k
    v_ref[...] = v
    kb_ref[...] = k.astype(BF16)
    vb_ref[...] = v.astype(BF16)
    lf = _log_sigmoid(jnp.dot(xb, wf_ref[...], preferred_element_type=F32) + bf_ref[...])
    lf_ref[...] = lf
    r = lax.broadcasted_iota(jnp.int32, (tm, tm), 0)
    c = lax.broadcasted_iota(jnp.int32, (tm, tm), 1)
    tri = (c <= r).astype(BF16)
    a1, a2, a3 = _split3(lf)
    cs = (jnp.dot(tri, a1, preferred_element_type=F32) + jnp.dot(tri, a2, preferred_element_type=F32)
          + jnp.dot(tri, a3, preferred_element_type=F32)) + carry[...]
    c_ref[...] = cs
    carry[...] = cs[tm - 1:tm, :]


def _kvf(x, wkv_bf, wf_bf, bf_pad, nb, tm):
    n = x.shape[0]
    t = n // nb
    assert t % tm == 0
    steps = t // tm
    row = lambda w: pl.BlockSpec((tm, w), lambda b, i: (b * steps + i, 0))
    full = lambda a: pl.BlockSpec(a.shape, lambda b, i: (0,) * a.ndim)
    return pl.pallas_call(
        _kvf_body,
        grid=(nb, steps),
        in_specs=[row(D_MODEL), full(wkv_bf), full(wf_bf), full(bf_pad)],
        out_specs=[row(MAIN_W), row(MAIN_W), row(MAIN_W), row(MAIN_W), row(LANES), row(LANES)],
        out_shape=[jax.ShapeDtypeStruct((n, MAIN_W), F32), jax.ShapeDtypeStruct((n, MAIN_W), F32),
                   jax.ShapeDtypeStruct((n, MAIN_W), BF16), jax.ShapeDtypeStruct((n, MAIN_W), BF16),
                   jax.ShapeDtypeStruct((n, LANES), F32), jax.ShapeDtypeStruct((n, LANES), F32)],
        scratch_shapes=[pltpu.VMEM((1, LANES), F32)],
        compiler_params=_cparams(("arbitrary", "arbitrary")),
        name="kvf",
    )(x, wkv_bf, wf_bf, bf_pad)


def _fox_body(q_ref, k_ref, v_ref, cq_ref, ck_ref, o_ref, m_scr, l_scr, acc_scr, *, tq):
    i = pl.program_id(2)
    q = q_ref[0, 0]
    cq = cq_ref[0, 0]
    m_scr[...] = jnp.full_like(m_scr, NEG_INF)
    l_scr[...] = jnp.zeros_like(l_scr)
    acc_scr[...] = jnp.zeros_like(acc_scr)

    def block(j, masked):
        start = pl.multiple_of(j * tq, tq)
        k = k_ref[0, 0, pl.ds(start, tq), :]
        v = v_ref[0, 0, pl.ds(start, tq), :]
        s = lax.dot_general(q, k, (((1,), (1,)), ((), ())), preferred_element_type=F32)
        s = s + (cq - ck_ref[0, 0, :, pl.ds(start, tq)])
        if masked:
            r = lax.broadcasted_iota(jnp.int32, (tq, tq), 0)
            c = lax.broadcasted_iota(jnp.int32, (tq, tq), 1)
            s = jnp.where(c <= r, s, NEG_INF)
        m_prev = m_scr[...]
        m_new = jnp.maximum(m_prev, jnp.max(s, axis=1, keepdims=True))
        alpha = jnp.exp(m_prev - m_new)
        p = jnp.exp(s - m_new)
        l_scr[...] = alpha * l_scr[...] + jnp.sum(p, axis=1, keepdims=True)
        acc_scr[...] = alpha * acc_scr[...] + jnp.dot(p.astype(BF16), v, preferred_element_type=F32)
        m_scr[...] = m_new

    def body(j, carry):
        block(j, False)
        return carry

    lax.fori_loop(0, i, body, 0)
    block(i, True)
    o_ref[0, 0] = (acc_scr[...] / l_scr[...]).astype(o_ref.dtype)


def _fox_prompt(q, k, v, c_col, c_row, tq):
    b, h, t, dh = q.shape
    assert t % tq == 0
    return pl.pallas_call(
        functools.partial(_fox_body, tq=tq),
        grid=(b, h, t // tq),
        in_specs=[pl.BlockSpec((1, 1, tq, dh), lambda bi, hi, i: (bi, hi, i, 0)),
                  pl.BlockSpec((1, 1, t, dh), lambda bi, hi, i: (bi, hi, 0, 0)),
                  pl.BlockSpec((1, 1, t, dh), lambda bi, hi, i: (bi, hi, 0, 0)),
                  pl.BlockSpec((1, 1, tq, 1), lambda bi, hi, i: (bi, hi, i, 0)),
                  pl.BlockSpec((1, 1, 1, t), lambda bi, hi, i: (bi, hi, 0, 0))],
        out_specs=pl.BlockSpec((1, 1, tq, dh), lambda bi, hi, i: (bi, hi, i, 0)),
        out_shape=jax.ShapeDtypeStruct((b, h, t, dh), BF16),
        scratch_shapes=[pltpu.VMEM((tq, 1), F32), pltpu.VMEM((tq, 1), F32), pltpu.VMEM((tq, dh), F32)],
        compiler_params=_cparams(("arbitrary", "arbitrary", "arbitrary")),
        name="fox_prompt",
    )(q, k, v, c_col, c_row)


DEC_PAGES_PER_STEP = 8
DEC_HEADS_PAD = 16


def _fox_decode_body(pt_ref, q_ref, kn_ref, vn_ref, lfn_ref, *refs, npp):
    del pt_ref
    k_refs = refs[:npp]
    v_refs = refs[npp:2 * npp]
    lf_refs = refs[2 * npp:3 * npp]
    o_ref = refs[3 * npp]
    m_scr, l_scr, acc_scr, c_scr = refs[3 * npp + 1:]
    j = pl.program_id(1)
    nj = pl.num_programs(1)
    hp = DEC_HEADS_PAD

    @pl.when(j == 0)
    def _():
        m_scr[...] = jnp.full_like(m_scr, NEG_INF)
        l_scr[...] = jnp.zeros_like(l_scr)
        acc_scr[...] = jnp.zeros_like(acc_scr)
        c_scr[...] = jnp.zeros_like(c_scr)

    q = q_ref[0]
    r = lax.broadcasted_iota(jnp.int32, (PAGE_SIZE, PAGE_SIZE), 0)
    c = lax.broadcasted_iota(jnp.int32, (PAGE_SIZE, PAGE_SIZE), 1)
    tri = (r <= c).astype(BF16)
    head = lax.broadcasted_iota(jnp.int32, (hp, 1), 0)

    for pg in range(npp):
        a1, a2, a3 = _split3(lf_refs[pg][0])
        cs = (jnp.dot(a1, tri, preferred_element_type=F32) + jnp.dot(a2, tri, preferred_element_type=F32)
              + jnp.dot(a3, tri, preferred_element_type=F32)) + c_scr[...]
        s = jnp.zeros((hp, PAGE_SIZE), F32)
        for h in range(FOX_H):
            qh = jnp.broadcast_to(q[h:h + 1, :], (8, FOX_DH)).astype(BF16)
            kh = k_refs[pg][0, :, h, :].astype(BF16)
            sh = lax.dot_general(qh, kh, (((1,), (1,)), ((), ())), preferred_element_type=F32)
            s = jnp.where(head == h, jnp.concatenate([sh, sh], axis=0), s)
        s = s - cs
        m_prev = m_scr[...]
        m_new = jnp.maximum(m_prev, jnp.max(s, axis=1, keepdims=True))
        alpha = jnp.exp(m_prev - m_new)
        p = jnp.exp(s - m_new)
        l_scr[...] = alpha * l_scr[...] + jnp.sum(p, axis=1, keepdims=True)
        pv = jnp.zeros((hp, FOX_DH), F32)
        for h in range(FOX_H):
            ph = jnp.broadcast_to(p[h:h + 1, :], (8, PAGE_SIZE)).astype(BF16)
            vh = v_refs[pg][0, :, h, :].astype(BF16)
            o = jnp.dot(ph, vh, preferred_element_type=F32)
            pv = jnp.where(head == h, jnp.concatenate([o, o], axis=0), pv)
        acc_scr[...] = alpha * acc_scr[...] + pv
        m_scr[...] = m_new
        c_scr[...] = cs[:, PAGE_SIZE - 1:PAGE_SIZE]

    @pl.when(j == nj - 1)
    def _():
        s_new = jnp.sum(q * kn_ref[0], axis=1, keepdims=True) - (c_scr[...] + lfn_ref[0])
        m_prev = m_scr[...]
        m_new = jnp.maximum(m_prev, s_new)
        alpha = jnp.exp(m_prev - m_new)
        p = jnp.exp(s_new - m_new)
        l = alpha * l_scr[...] + p
        acc = alpha * acc_scr[...] + p * vn_ref[0]
        o_ref[0] = acc / l


def _fox_decode(page_table, q, k_new, v_new, lf_new, cache_k, cache_v, cache_lf_t):
    b, n_pages = page_table.shape
    npp = DEC_PAGES_PER_STEP
    hp = DEC_HEADS_PAD
    assert n_pages % npp == 0
    tok = lambda w: pl.BlockSpec((1, hp, w), lambda bi, j, pt: (bi, 0, 0))

    def page_spec(shape_tail, pg):
        nd = len(shape_tail)
        return pl.BlockSpec((1,) + shape_tail, lambda bi, j, pt, pg=pg: (pt[bi, j * npp + pg],) + (0,) * nd)

    in_specs = [tok(FOX_DH), tok(FOX_DH), tok(FOX_DH), tok(1)]
    in_specs += [page_spec((PAGE_SIZE, FOX_H, FOX_DH), pg) for pg in range(npp)]
    in_specs += [page_spec((PAGE_SIZE, FOX_H, FOX_DH), pg) for pg in range(npp)]
    in_specs += [page_spec((hp, PAGE_SIZE), pg) for pg in range(npp)]
    grid_spec = pltpu.PrefetchScalarGridSpec(
        num_scalar_prefetch=1,
        grid=(b, n_pages // npp),
        in_specs=in_specs,
        out_specs=pl.BlockSpec((1, hp, FOX_DH), lambda bi, j, pt: (bi, 0, 0)),
        scratch_shapes=[pltpu.VMEM((hp, 1), F32), pltpu.VMEM((hp, 1), F32),
                        pltpu.VMEM((hp, FOX_DH), F32), pltpu.VMEM((hp, 1), F32)],
    )
    return pl.pallas_call(
        functools.partial(_fox_decode_body, npp=npp),
        grid_spec=grid_spec,
        out_shape=jax.ShapeDtypeStruct((b, hp, FOX_DH), F32),
        compiler_params=_cparams(("arbitrary", "arbitrary")),
        name="fox_decode",
    )(page_table, q, k_new, v_new, lf_new, *([cache_k] * npp), *([cache_v] * npp), *([cache_lf_t] * npp))


def _heads(x, b, t):
    return x.reshape(b, t, FOX_H, FOX_DH).transpose(0, 2, 1, 3)


def kernel(x_prompt, x_sample, cache_k, cache_v, cache_logf, cache_mem_k, cache_mem_v, state_ssm_re, state_ssm_im, page_table, mem_prompt, w_in, w_out, ln1_g, ln1_b, ln2_g, ln2_b, w_mem_kv, ssm_a_re, ssm_a_im, ssm_log_dt, ssm_b_re, ssm_b_im, ssm_c_re, ssm_c_im, ssm_d, w_glu, b_glu, w_kvf, b_f, w_r1, b_r1, w_r2, b_r2, w_e_gate, w_e_up, w_e_down):
    bp, tp, _ = x_prompt.shape
    bs = x_sample.shape[0]
    n_p = bp * tp
    n_mem = mem_prompt.shape[1]
    tm = 512

    w_in_bf = w_in.astype(BF16)
    w_out_bf = w_out.astype(BF16)
    w_glu_bf = w_glu.astype(BF16)
    wkv_bf = w_kvf[:, :2 * MAIN_W].astype(BF16)
    wf_bf = jnp.pad(w_kvf[:, 2 * MAIN_W:], ((0, 0), (0, LANES - FOX_H))).astype(BF16)
    bf_pad = jnp.pad(b_f, (0, LANES - FOX_H))[None, :]
    n_e = MOE_GROUPS * MOE_EPG
    wr = jnp.concatenate([w_r1, w_r2.transpose(0, 2, 1, 3).reshape(DEPTH, D_MODEL, n_e)], axis=-1)
    wr_bf = jnp.pad(wr, ((0, 0), (0, 0), (0, LANES - MOE_GROUPS - n_e))).astype(BF16)
    br = jnp.pad(jnp.concatenate([b_r1, b_r2.reshape(DEPTH, n_e)], axis=-1),
                 ((0, 0), (0, LANES - MOE_GROUPS - n_e)))[:, None, :]
    hf = MOE_EPG * MOE_F

    def group_cols(w):
        return w.reshape(DEPTH, MOE_GROUPS, MOE_EPG, D_MODEL, MOE_F).transpose(0, 1, 3, 2, 4).reshape(
            DEPTH, MOE_GROUPS, D_MODEL, hf)

    wgu_bf = jnp.concatenate([group_cols(w_e_gate), group_cols(w_e_up)], axis=-1).astype(BF16)
    wd_bf = w_e_down.reshape(DEPTH, MOE_GROUPS, hf, D_MODEL).astype(BF16)
    sw = _s5_weights(ssm_a_re[0], ssm_a_im[0], ssm_log_dt[0], ssm_b_re[0], ssm_b_im[0], ssm_c_re[0], ssm_c_im[0])

    mem2d = mem_prompt.reshape(bp * n_mem, D_MODEL)
    p_mem = [_proj(mem2d, w_mem_kv[l].astype(BF16), (MEM_W, MEM_W), (F32, F32), tm) for l in range(DEPTH)]
    p_mem_k = jnp.stack([pm[0].reshape(bp, n_mem, MEM_H, MEM_DH) for pm in p_mem])
    p_mem_v = jnp.stack([pm[1].reshape(bp, n_mem, MEM_H, MEM_DH) for pm in p_mem])

    xp = x_prompt.reshape(n_p, D_MODEL)
    xs = x_sample.reshape(bs, D_MODEL)

    up, qmp = _proj(xp, w_in_bf[0], (MAIN_W, MEM_W), (F32, BF16), tm)
    us, qms = _proj(xs, w_in_bf[0], (MAIN_W, MEM_W), (F32, BF16), bs)
    zp, hfin = _s5_prompt(up.reshape(bp, tp, MAIN_W), sw, ssm_d, 2048)
    p_ssm_re = hfin[:, :, 0, :S5_STATE].reshape(1, bp, SSM_G, SSM_P)
    p_ssm_im = hfin[:, :, 0, S5_STATE:].reshape(1, bp, SSM_G, SSM_P)

    def lt_state(s):
        return s.reshape(bs, S5_NLT, S5_STATE).transpose(1, 0, 2)

    h0 = jnp.concatenate([lt_state(state_ssm_re[0]), lt_state(state_ssm_im[0])], axis=-1)
    zs, hs = _s5_step(us, h0, sw, ssm_d)
    s_ssm_re = hs[:, :, :S5_STATE].transpose(1, 0, 2).reshape(1, bs, SSM_G, SSM_P)
    s_ssm_im = hs[:, :, S5_STATE:].transpose(1, 0, 2).reshape(1, bs, SSM_G, SSM_P)

    def mem_kv_bf(x, b):
        return x.reshape(b, n_mem, MEM_W).astype(BF16)

    memp = _mem_attn(qmp.reshape(bp, tp, MEM_W), mem_kv_bf(p_mem_k[0], bp), mem_kv_bf(p_mem_v[0], bp), tm)
    mems = _mem_attn(qms.reshape(bs, 1, MEM_W), mem_kv_bf(cache_mem_k[0], bs), mem_kv_bf(cache_mem_v[0], bs), 1)

    x1p, rp = _mix(xp, zp.reshape(n_p, MAIN_W), memp.reshape(n_p, MEM_W), w_out_bf[0], ln1_g[0:1], ln1_b[0:1],
                   wr_bf[0], br[0], tm, w_glu_bf[0], b_glu[0:1])
    x1s, rs = _mix(xs, zs, mems.reshape(bs, MEM_W), w_out_bf[0], ln1_g[0:1], ln1_b[0:1],
                   wr_bf[0], br[0], bs, w_glu_bf[0], b_glu[0:1])
    x2p, x2s = _moe_layer([x1p, x1s], [rp, rs], wgu_bf[0], wd_bf[0], ln2_g[0:1], ln2_b[0:1])

    kp, vp, kpb, vpb, lfp, cp = _kvf(x2p, wkv_bf, wf_bf, bf_pad, bp, tm)
    ks, vs, _, _, lfs, _ = _kvf(x2s, wkv_bf, wf_bf, bf_pad, 1, bs)
    p_k = kp.reshape(bp, tp, FOX_H, FOX_DH)
    p_v = vp.reshape(bp, tp, FOX_H, FOX_DH)
    p_logf = lfp[:, :FOX_H].reshape(bp, tp, FOX_H)
    s_k = ks.reshape(bs, 1, FOX_H, FOX_DH)
    s_v = vs.reshape(bs, 1, FOX_H, FOX_DH)
    s_logf = lfs[:, :FOX_H].reshape(bs, 1, FOX_H)

    qp, qmp = _proj(x2p, w_in_bf[1], (MAIN_W, MEM_W), (BF16, BF16), tm)
    qs, qms = _proj(x2s, w_in_bf[1], (MAIN_W, MEM_W), (F32, BF16), bs)
    scale = FOX_DH ** -0.5
    c_hm = cp[:, :FOX_H].reshape(bp, tp, FOX_H).transpose(0, 2, 1)
    attn = _fox_prompt(_heads(qp * jnp.asarray(scale, BF16), bp, tp), _heads(kpb, bp, tp), _heads(vpb, bp, tp),
                       c_hm[..., None], c_hm[:, :, None, :], 512)
    attn_p = attn.transpose(0, 2, 1, 3).reshape(n_p, MAIN_W)
    hpad = DEC_HEADS_PAD - FOX_H

    def pad_heads(x):
        return jnp.pad(x.reshape(bs, FOX_H, FOX_DH), ((0, 0), (0, hpad), (0, 0)))

    cache_lf_t = jnp.pad(cache_logf.transpose(0, 2, 1), ((0, 0), (0, hpad), (0, 0)))
    attn_s = _fox_decode(page_table, pad_heads(qs * scale), pad_heads(ks), pad_heads(vs),
                         lfs[:, :DEC_HEADS_PAD, None] * (jnp.arange(DEC_HEADS_PAD) < FOX_H)[None, :, None],
                         cache_k, cache_v, cache_lf_t)[:, :FOX_H, :].reshape(bs, MAIN_W)

    memp = _mem_attn(qmp.reshape(bp, tp, MEM_W), mem_kv_bf(p_mem_k[1], bp), mem_kv_bf(p_mem_v[1], bp), tm)
    mems = _mem_attn(qms.reshape(bs, 1, MEM_W), mem_kv_bf(cache_mem_k[1], bs), mem_kv_bf(cache_mem_v[1], bs), 1)
    x1p, rp = _mix(x2p, attn_p, memp.reshape(n_p, MEM_W), w_out_bf[1], ln1_g[1:2], ln1_b[1:2], wr_bf[1], br[1], tm)
    x1s, rs = _mix(x2s, attn_s, mems.reshape(bs, MEM_W), w_out_bf[1], ln1_g[1:2], ln1_b[1:2], wr_bf[1], br[1], bs)
    yp, ys = _moe_layer([x1p, x1s], [rp, rs], wgu_bf[1], wd_bf[1], ln2_g[1:2], ln2_b[1:2])

    return (yp.reshape(bp, tp, D_MODEL), ys.reshape(bs, 1, D_MODEL), p_ssm_re, p_ssm_im, p_k, p_v, p_logf,
            p_mem_k, p_mem_v, s_ssm_re, s_ssm_im, s_k, s_v, s_logf)
```

```python
import functools
import math

import jax
import jax.numpy as jnp
from jax import lax
from jax.experimental import pallas as pl
from jax.experimental.pallas import tpu as pltpu

F32 = jnp.float32
BF16 = jnp.bfloat16

D_MODEL = 1024
DEPTH = 2
MAIN_W = 768
MEM_W = 256
SSM_GC = 16
SSM_G = 48
SSM_P = 64
FOX_DH = 64
FOX_H = 12
MEM_H = 4
MEM_DH = 64
PAGE_SIZE = 128
MOE_GROUPS = 4
MOE_EPG = 4
MOE_F = 256
DN_ALPHA = (2.0 * DEPTH) ** 0.25
LN_EPS = 1e-5
NEG_INF = -1e30

LANES = 128
S5_CHUNK = 8
S5_GPL = LANES // SSM_GC
S5_NLT = MAIN_W // LANES
S5_STATE = S5_GPL * SSM_P
VMEM_LIMIT = 56 * 1024 * 1024


def _cparams(sem):
    return pltpu.CompilerParams(dimension_semantics=sem, vmem_limit_bytes=VMEM_LIMIT)


def _proj_body(x_ref, w_ref, *o_refs, widths):
    y = jnp.dot(x_ref[...].astype(BF16), w_ref[...], preferred_element_type=F32)
    off = 0
    for o_ref, n in zip(o_refs, widths):
        o_ref[...] = y[:, off:off + n].astype(o_ref.dtype)
        off += n


def _proj(x, w_bf, widths, dtypes, tm):
    m, k = x.shape
    n = w_bf.shape[1]
    assert m % tm == 0 and sum(widths) == n
    return pl.pallas_call(
        functools.partial(_proj_body, widths=tuple(widths)),
        grid=(m // tm,),
        in_specs=[pl.BlockSpec((tm, k), lambda i: (i, 0)),
                  pl.BlockSpec((k, n), lambda i: (0, 0))],
        out_specs=[pl.BlockSpec((tm, w), lambda i: (i, 0)) for w in widths],
        out_shape=[jax.ShapeDtypeStruct((m, w), dt) for w, dt in zip(widths, dtypes)],
        compiler_params=_cparams(("arbitrary",)),
        name="proj",
    )(x, w_bf)


def _gelu(x):
    c = math.sqrt(2.0 / math.pi)
    return 0.5 * x * (1.0 + jnp.tanh(c * (x + 0.044715 * (x * x * x))))


def _s5_weights(a_re, a_im, log_dt, b_re, b_im, c_re, c_im):
    f32 = F32
    L = S5_CHUNK
    a_re, a_im = a_re.astype(f32), a_im.astype(f32)
    dt = jnp.exp(log_dt.astype(f32))[:, None]
    mag = jnp.exp(a_re * dt)
    lb_re = mag * jnp.cos(a_im * dt)
    lb_im = mag * jnp.sin(a_im * dt)
    den = a_re * a_re + a_im * a_im
    zr = ((lb_re - 1.0) * a_re + lb_im * a_im) / den
    zi = (lb_im * a_re - (lb_re - 1.0) * a_im) / den
    b_re, b_im = b_re.astype(f32), b_im.astype(f32)
    bb_re = zr[..., None] * b_re - zi[..., None] * b_im
    bb_im = zr[..., None] * b_im + zi[..., None] * b_re
    ks = jnp.arange(L + 1, dtype=f32)[:, None, None]
    mag_k = jnp.exp(a_re * dt * ks)
    pk_re = mag_k * jnp.cos(a_im * dt * ks)
    pk_im = mag_k * jnp.sin(a_im * dt * ks)
    c_re, c_im = c_re.astype(f32), c_im.astype(f32)

    eye = jnp.eye(S5_GPL, dtype=f32)

    pb_re = pk_re[..., None] * bb_re - pk_im[..., None] * bb_im
    pb_im = pk_re[..., None] * bb_im + pk_im[..., None] * bb_re

    def lt_w1(pb):
        w = pb[L - 1 - jnp.arange(L)]
        w = w.transpose(1, 0, 3, 2).reshape(S5_NLT, S5_GPL, L, SSM_GC, SSM_P)
        w = w.transpose(0, 2, 1, 3, 4)[:, :, :, :, None, :] * eye[None, None, :, None, :, None]
        return w.reshape(S5_NLT, L * LANES, S5_STATE)

    w1 = jnp.concatenate([lt_w1(pb_re), lt_w1(pb_im)], axis=-1)

    kk = (jnp.sum(pb_re[:L].transpose(0, 1, 3, 2)[:, :, :, None, :] * c_re[None, :, None, :, :], axis=-1)
          - jnp.sum(pb_im[:L].transpose(0, 1, 3, 2)[:, :, :, None, :] * c_im[None, :, None, :, :], axis=-1))
    ti = jnp.arange(L)[:, None]
    to = jnp.arange(L)[None, :]
    lag = jnp.clip(to - ti, 0, L - 1)
    mm = jnp.where((to >= ti)[:, :, None, None, None], kk[lag], 0.0)
    mm = mm.transpose(2, 0, 3, 1, 4).reshape(S5_NLT, S5_GPL, L, SSM_GC, L, SSM_GC)
    mm = mm.transpose(0, 2, 1, 3, 4, 5)[:, :, :, :, :, None, :] * eye[None, None, :, None, None, :, None]
    mm = mm.reshape(S5_NLT, L * LANES, L * LANES)

    def lt_w2(x):
        w = x.transpose(1, 3, 0, 2).reshape(S5_NLT, S5_GPL, SSM_P, L, SSM_GC)
        w = w[:, :, :, :, None, :] * eye[None, :, None, None, :, None]
        return w.reshape(S5_NLT, S5_STATE, L * LANES)

    ca_re = c_re[None] * pk_re[1:, :, None, :] - c_im[None] * pk_im[1:, :, None, :]
    ca_im = c_re[None] * pk_im[1:, :, None, :] + c_im[None] * pk_re[1:, :, None, :]
    w2 = jnp.concatenate([lt_w2(ca_re), lt_w2(-ca_im)], axis=1)

    def lt_vec(x):
        return x.reshape(S5_NLT, 1, S5_STATE)

    a_chunk = jnp.concatenate([lt_vec(pk_re[L]), lt_vec(pk_im[L])], axis=-1)
    a_one = jnp.concatenate([lt_vec(pk_re[1]), lt_vec(pk_im[1])], axis=-1)

    def lt_c0(x):
        w = x.transpose(0, 2, 1).reshape(S5_NLT, S5_GPL, SSM_P, SSM_GC)
        w = w[:, :, :, None, :] * eye[None, :, None, :, None]
        return w.reshape(S5_NLT, S5_STATE, LANES)

    c0 = jnp.concatenate([lt_c0(c_re), lt_c0(-c_im)], axis=1)
    bb1 = w1[:, (L - 1) * LANES:, :]
    return dict(w1m=jnp.concatenate([w1, mm], axis=-1).astype(BF16), w2=w2.astype(BF16),
                a_chunk=a_chunk, a_one=a_one, c0=c0.astype(BF16), bb1=bb1.astype(BF16))


def _s5_body(u_ref, w1m_ref, w2_ref, ach_ref, d_ref, z_ref, hfin_ref, s_scr, hp_scr, h_scr, *, tm):
    t = pl.program_id(2)
    nc = tm // S5_CHUNK
    ns = S5_STATE

    @pl.when(t == 0)
    def _():
        h_scr[...] = jnp.zeros_like(h_scr)

    parts = [u_ref[0, pl.ds(k, nc, stride=S5_CHUNK), :] for k in range(S5_CHUNK)]
    a = jnp.concatenate(parts, axis=1).astype(BF16)
    sy = jnp.dot(a, w1m_ref[0], preferred_element_type=F32)
    s_scr[...] = sy[:, :2 * ns]
    ar = ach_ref[0, :, :ns]
    ai = ach_ref[0, :, ns:]

    def step(j, carry):
        hr, hi = carry
        hp_scr[pl.ds(j, 1), :ns] = hr
        hp_scr[pl.ds(j, 1), ns:] = hi
        sr = s_scr[pl.ds(j, 1), :ns]
        si = s_scr[pl.ds(j, 1), ns:]
        return ar * hr - ai * hi + sr, ar * hi + ai * hr + si

    hr, hi = lax.fori_loop(0, nc, step, (h_scr[:, :ns], h_scr[:, ns:]), unroll=8)
    h_scr[:, :ns] = hr
    h_scr[:, ns:] = hi
    hfin_ref[0, 0] = h_scr[...]

    yi = jnp.dot(hp_scr[...].astype(BF16), w2_ref[0], preferred_element_type=F32)
    d = d_ref[...]
    for k in range(S5_CHUNK):
        yk = sy[:, 2 * ns + k * LANES:2 * ns + (k + 1) * LANES] + yi[:, k * LANES:(k + 1) * LANES] + d * parts[k]
        z_ref[0, pl.ds(k, nc, stride=S5_CHUNK), :] = _gelu(yk)


def _s5_prompt(u, sw, d_skip, tm):
    b, t, _ = u.shape
    assert t % tm == 0 and tm % (8 * S5_CHUNK) == 0
    nc = tm // S5_CHUNK
    ns2 = 2 * S5_STATE
    kw = S5_CHUNK * LANES
    return pl.pallas_call(
        functools.partial(_s5_body, tm=tm),
        grid=(S5_NLT, b, t // tm),
        in_specs=[pl.BlockSpec((1, tm, LANES), lambda l, bb, tt: (bb, tt, l)),
                  pl.BlockSpec((1, kw, ns2 + kw), lambda l, bb, tt: (l, 0, 0)),
                  pl.BlockSpec((1, ns2, kw), lambda l, bb, tt: (l, 0, 0)),
                  pl.BlockSpec((1, 1, ns2), lambda l, bb, tt: (l, 0, 0)),
                  pl.BlockSpec((1, LANES), lambda l, bb, tt: (0, l))],
        out_specs=[pl.BlockSpec((1, tm, LANES), lambda l, bb, tt: (bb, tt, l)),
                   pl.BlockSpec((1, 1, 1, ns2), lambda l, bb, tt: (bb, l, 0, 0))],
        out_shape=[jax.ShapeDtypeStruct(u.shape, F32),
                   jax.ShapeDtypeStruct((b, S5_NLT, 1, ns2), F32)],
        scratch_shapes=[pltpu.VMEM((nc, ns2), F32), pltpu.VMEM((nc, ns2), F32), pltpu.VMEM((1, ns2), F32)],
        compiler_params=_cparams(("arbitrary", "arbitrary", "arbitrary")),
        name="s5_prompt",
    )(u, sw["w1m"], sw["w2"], sw["a_chunk"], d_skip)


def _s5_step_body(u_ref, h0_ref, bb1_ref, c0_ref, a1_ref, d_ref, z_ref, h_ref):
    ns = S5_STATE
    for l in range(S5_NLT):
        u = u_ref[:, l * LANES:(l + 1) * LANES]
        bu = jnp.dot(u.astype(BF16), bb1_ref[l], preferred_element_type=F32)
        ar = a1_ref[l, :, :ns]
        ai = a1_ref[l, :, ns:]
        h0r = h0_ref[l, :, :ns]
        h0i = h0_ref[l, :, ns:]
        hr = ar * h0r - ai * h0i + bu[:, :ns]
        hi = ar * h0i + ai * h0r + bu[:, ns:]
        h_ref[l, :, :ns] = hr
        h_ref[l, :, ns:] = hi
        h = jnp.concatenate([hr, hi], axis=1).astype(BF16)
        y = jnp.dot(h, c0_ref[l], preferred_element_type=F32) + d_ref[:, l * LANES:(l + 1) * LANES] * u
        z_ref[:, l * LANES:(l + 1) * LANES] = _gelu(y)


def _s5_step(u, h0, sw, d_skip):
    n = u.shape[0]
    return pl.pallas_call(
        _s5_step_body,
        out_shape=[jax.ShapeDtypeStruct((n, MAIN_W), F32),
                   jax.ShapeDtypeStruct((S5_NLT, n, 2 * S5_STATE), F32)],
        compiler_params=pltpu.CompilerParams(vmem_limit_bytes=VMEM_LIMIT),
        name="s5_step",
    )(u, h0, sw["bb1"], sw["c0"], sw["a_one"], d_skip)


def _mem_attn_body(q_ref, k_ref, v_ref, o_ref):
    q = q_ref[0]
    tm = q.shape[0]
    rows = max(tm, 8)
    outs = []
    for h in range(MEM_H):
        sl = slice(h * MEM_DH, (h + 1) * MEM_DH)
        qh = q[:, sl]
        if tm < rows:
            qh = jnp.broadcast_to(qh, (rows, MEM_DH))
        s = lax.dot_general(qh, k_ref[0, :, sl], (((1,), (1,)), ((), ())),
                            preferred_element_type=F32) * (MEM_DH ** -0.5)
        m = jnp.max(s, axis=1, keepdims=True)
        e = jnp.exp(s - m)
        p = e / jnp.sum(e, axis=1, keepdims=True)
        o = jnp.dot(p.astype(BF16), v_ref[0, :, sl], preferred_element_type=F32)
        outs.append(o[:tm])
    o_ref[0] = jnp.concatenate(outs, axis=1).astype(o_ref.dtype)


def _mem_attn(q, k, v, tm):
    b, t, w = q.shape
    nm = k.shape[1]
    return pl.pallas_call(
        _mem_attn_body,
        grid=(b, t // tm),
        in_specs=[pl.BlockSpec((1, tm, w), lambda i, j: (i, j, 0)),
                  pl.BlockSpec((1, nm, w), lambda i, j: (i, 0, 0)),
                  pl.BlockSpec((1, nm, w), lambda i, j: (i, 0, 0))],
        out_specs=pl.BlockSpec((1, tm, w), lambda i, j: (i, j, 0)),
        out_shape=jax.ShapeDtypeStruct((b, t, w), BF16),
        compiler_params=_cparams(("arbitrary", "arbitrary")),
        name="mem_attn",
    )(q, k, v)


def _layer_norm(x, g, b):
    mu = jnp.mean(x, axis=-1, keepdims=True)
    xc = x - mu
    var = jnp.mean(xc * xc, axis=-1, keepdims=True)
    return xc * lax.rsqrt(var + LN_EPS) * g + b


def _route(logits):
    lane = lax.broadcasted_iota(jnp.int32, logits.shape, 1).astype(F32)
    lo = jnp.float32(jnp.finfo(jnp.float32).min)
    none = jnp.float32(LANES)
    is_g = lane < MOE_GROUPS
    gl = jnp.where(is_g, logits, lo)
    gm = jnp.max(gl, axis=1, keepdims=True)
    gsum = jnp.sum(jnp.where(is_g, jnp.exp(gl - gm), 0.0), axis=1, keepdims=True)
    p_g = 1.0 / gsum
    g_idx = jnp.min(jnp.where(is_g & (gl == gm), lane, none), axis=1, keepdims=True)
    e_lo = MOE_GROUPS + MOE_EPG * g_idx
    is_e = (lane >= e_lo) & (lane < e_lo + MOE_EPG)
    el = jnp.where(is_e, logits, lo)
    v1 = jnp.max(el, axis=1, keepdims=True)
    i1 = jnp.min(jnp.where(is_e & (el == v1), lane, none), axis=1, keepdims=True)
    is_e2 = is_e & (lane != i1)
    el2 = jnp.where(is_e2, logits, lo)
    v2 = jnp.max(el2, axis=1, keepdims=True)
    i2 = jnp.min(jnp.where(is_e2 & (el2 == v2), lane, none), axis=1, keepdims=True)
    r = jnp.exp(v2 - v1)
    w1 = p_g / (1.0 + r)
    w2 = p_g * r / (1.0 + r)
    out = jnp.where(lane == i1, w1, 0.0) + jnp.where(lane == i2, w2, 0.0)
    return out + jnp.where(lane == 0.0, g_idx, 0.0)


def _mix_body(*refs, glu):
    if glu:
        (x_ref, main_ref, mem_ref, wglu_ref, bglu_ref, wout_ref, g_ref, b_ref, wr_ref, br_ref,
         x1_ref, route_ref) = refs
        z = main_ref[...]
        gate = jnp.dot(z.astype(BF16), wglu_ref[...], preferred_element_type=F32) + bglu_ref[...]
        main = (z * jax.nn.sigmoid(gate)).astype(BF16)
    else:
        x_ref, main_ref, mem_ref, wout_ref, g_ref, b_ref, wr_ref, br_ref, x1_ref, route_ref = refs
        main = main_ref[...].astype(BF16)
    mixed = (jnp.dot(main, wout_ref[:MAIN_W, :], preferred_element_type=F32)
             + jnp.dot(mem_ref[...], wout_ref[MAIN_W:, :], preferred_element_type=F32))
    x1 = _layer_norm(DN_ALPHA * x_ref[...] + mixed, g_ref[...], b_ref[...])
    x1_ref[...] = x1
    logits = jnp.dot(x1.astype(BF16), wr_ref[...], preferred_element_type=F32) + br_ref[...]
    route_ref[...] = _route(logits)


def _mix(x, main, mem, w_out_bf, ln_g, ln_b, wr_bf, br, tm, w_glu_bf=None, b_glu=None):
    n = x.shape[0]
    glu = w_glu_bf is not None
    row = lambda w: pl.BlockSpec((tm, w), lambda i: (i, 0))
    full = lambda a: pl.BlockSpec(a.shape, lambda i: (0,) * a.ndim)
    args = [x, main, mem]
    specs = [row(D_MODEL), row(MAIN_W), row(MEM_W)]
    if glu:
        args += [w_glu_bf, b_glu]
        specs += [full(w_glu_bf), full(b_glu)]
    tail = [w_out_bf, ln_g, ln_b, wr_bf, br]
    args += tail
    specs += [full(a) for a in tail]
    return pl.pallas_call(
        functools.partial(_mix_body, glu=glu),
        grid=(n // tm,),
        in_specs=specs,
        out_specs=[row(D_MODEL), row(LANES)],
        out_shape=[jax.ShapeDtypeStruct((n, D_MODEL), F32), jax.ShapeDtypeStruct((n, LANES), F32)],
        compiler_params=_cparams(("arbitrary",)),
        name="mix_glu" if glu else "mix",
    )(*args)


def _moe_body(tg_ref, x_ref, cw_ref, wgu_ref, wd_ref, g_ref, b_ref, o_ref):
    del tg_ref
    x = x_ref[...]
    hf = MOE_EPG * MOE_F
    gu = jnp.dot(x.astype(BF16), wgu_ref[0], preferred_element_type=F32)
    h = jax.nn.silu(gu[:, :hf]) * gu[:, hf:]
    cw = cw_ref[...]
    hs = [h[:, e * MOE_F:(e + 1) * MOE_F] * cw[:, e:e + 1] for e in range(MOE_EPG)]
    hb = jnp.concatenate(hs, axis=1).astype(BF16)
    y = jnp.dot(hb, wd_ref[0], preferred_element_type=F32)
    o_ref[...] = _layer_norm(DN_ALPHA * x + y, g_ref[...], b_ref[...])


def _moe(tile_group, xs, cws, wgu_bf, wd_bf, ln_g, ln_b, tm):
    p = xs.shape[0]
    hf = MOE_EPG * MOE_F
    grid_spec = pltpu.PrefetchScalarGridSpec(
        num_scalar_prefetch=1,
        grid=(p // tm,),
        in_specs=[pl.BlockSpec((tm, D_MODEL), lambda i, tg: (i, 0)),
                  pl.BlockSpec((tm, MOE_EPG), lambda i, tg: (i, 0)),
                  pl.BlockSpec((1, D_MODEL, 2 * hf), lambda i, tg: (tg[i], 0, 0)),
                  pl.BlockSpec((1, hf, D_MODEL), lambda i, tg: (tg[i], 0, 0)),
                  pl.BlockSpec((1, D_MODEL), lambda i, tg: (0, 0)),
                  pl.BlockSpec((1, D_MODEL), lambda i, tg: (0, 0))],
        out_specs=pl.BlockSpec((tm, D_MODEL), lambda i, tg: (i, 0)),
    )
    return pl.pallas_call(
        _moe_body,
        grid_spec=grid_spec,
        out_shape=jax.ShapeDtypeStruct((p, D_MODEL), F32),
        compiler_params=_cparams(("arbitrary",)),
        name="moe",
    )(tile_group, xs, cws, wgu_bf, wd_bf, ln_g, ln_b)


MOE_TM = 256


def _moe_layer(x1_parts, route_parts, wgu_bf, wd_bf, ln_g, ln_b):
    x1 = jnp.concatenate(x1_parts, axis=0)
    route = jnp.concatenate(route_parts, axis=0)
    n = x1.shape[0]
    tm = MOE_TM
    g_idx = route[:, 0].astype(jnp.int32)
    cw16 = route[:, MOE_GROUPS:MOE_GROUPS + MOE_GROUPS * MOE_EPG].reshape(n, MOE_GROUPS, MOE_EPG)
    cw = jnp.take_along_axis(cw16, g_idx[:, None, None], axis=1)[:, 0, :]
    onehot = (g_idx[:, None] == jnp.arange(MOE_GROUPS)[None, :]).astype(jnp.int32)
    counts = jnp.sum(onehot, axis=0)
    rank = jnp.sum((jnp.cumsum(onehot, axis=0) - onehot) * onehot, axis=1)
    tiles = (counts + tm - 1) // tm
    tile_start = jnp.cumsum(tiles) - tiles
    pos = tile_start[g_idx] * tm + rank
    n_tiles = (n + tm - 1) // tm + MOE_GROUPS
    p = n_tiles * tm
    src = jnp.full((p,), n, jnp.int32).at[pos].set(jnp.arange(n, dtype=jnp.int32))
    tile_end = jnp.cumsum(tiles)
    tile_group = jnp.minimum(jnp.sum((jnp.arange(n_tiles)[:, None] >= tile_end[None, :]).astype(jnp.int32), axis=1),
                             MOE_GROUPS - 1).astype(jnp.int32)
    xs = x1[jnp.minimum(src, n - 1)]
    cws = jnp.concatenate([cw, jnp.zeros((1, MOE_EPG), F32)], axis=0)[src]
    ys = _moe(tile_group, xs, cws, wgu_bf, wd_bf, ln_g, ln_b, tm)
    y = ys[pos]
    outs, off = [], 0
    for part in x1_parts:
        outs.append(y[off:off + part.shape[0]])
        off += part.shape[0]
    return outs


def _log_sigmoid(x):
    return -(jnp.maximum(-x, 0.0) + jnp.log1p(jnp.exp(-jnp.abs(x))))


def _split3(x):
    a1 = x.astype(BF16)
    r1 = x - a1.astype(F32)
    a2 = r1.astype(BF16)
    a3 = (r1 - a2.astype(F32)).astype(BF16)
    return a1, a2, a3


def _kvf_body(x_ref, wkv_ref, wf_ref, bf_ref, k_ref, v_ref, kb_ref, vb_ref, lf_ref, c_ref, carry):
    t = pl.program_id(1)

    @pl.when(t == 0)
    def _():
        carry[...] = jnp.zeros_like(carry)

    xb = x_ref[...].astype(BF16)
    tm = xb.shape[0]
    kv = jnp.dot(xb, wkv_ref[...], preferred_element_type=F32)
    k = kv[:, :MAIN_W]
    v = kv[:, MAIN_W:]
    k_ref[...] = k
    v_ref[...] = v
    kb_ref[...] = k.astype(BF16)
    vb_ref[...] = v.astype(BF16)
    lf = _log_sigmoid(jnp.dot(xb, wf_ref[...], preferred_element_type=F32) + bf_ref[...])
    lf_ref[...] = lf
    r = lax.broadcasted_iota(jnp.int32, (tm, tm), 0)
    c = lax.broadcasted_iota(jnp.int32, (tm, tm), 1)
    tri = (c <= r).astype(BF16)
    a1, a2, a3 = _split3(lf)
    cs = (jnp.dot(tri, a1, preferred_element_type=F32) + jnp.dot(tri, a2, preferred_element_type=F32)
          + jnp.dot(tri, a3, preferred_element_type=F32)) + carry[...]
    c_ref[...] = cs
    carry[...] = cs[tm - 1:tm, :]


def _kvf(x, wkv_bf, wf_bf, bf_pad, nb, tm):
    n = x.shape[0]
    t = n // nb
    assert t % tm == 0
    steps = t // tm
    row = lambda w: pl.BlockSpec((tm, w), lambda b, i: (b * steps + i, 0))
    full = lambda a: pl.BlockSpec(a.shape, lambda b, i: (0,) * a.ndim)
    return pl.pallas_call(
        _kvf_body,
        grid=(nb, steps),
        in_specs=[row(D_MODEL), full(wkv_bf), full(wf_bf), full(bf_pad)],
        out_specs=[row(MAIN_W), row(MAIN_W), row(MAIN_W), row(MAIN_W), row(LANES), row(LANES)],
        out_shape=[jax.ShapeDtypeStruct((n, MAIN_W), F32), jax.ShapeDtypeStruct((n, MAIN_W), F32),
                   jax.ShapeDtypeStruct((n, MAIN_W), BF16), jax.ShapeDtypeStruct((n, MAIN_W), BF16),
                   jax.ShapeDtypeStruct((n, LANES), F32), jax.ShapeDtypeStruct((n, LANES), F32)],
        scratch_shapes=[pltpu.VMEM((1, LANES), F32)],
        compiler_params=_cparams(("arbitrary", "arbitrary")),
        name="kvf",
    )(x, wkv_bf, wf_bf, bf_pad)


def _fox_body(q_ref, k_ref, v_ref, cq_ref, ck_ref, o_ref, m_scr, l_scr, acc_scr, *, tq):
    i = pl.program_id(2)
    q = q_ref[0, 0]
    cq = cq_ref[0, 0]
    m_scr[...] = jnp.full_like(m_scr, NEG_INF)
    l_scr[...] = jnp.zeros_like(l_scr)
    acc_scr[...] = jnp.zeros_like(acc_scr)

    def block(j, masked):
        start = pl.multiple_of(j * tq, tq)
        k = k_ref[0, 0, pl.ds(start, tq), :]
        v = v_ref[0, 0, pl.ds(start, tq), :]
        s = lax.dot_general(q, k, (((1,), (1,)), ((), ())), preferred_element_type=F32)
        s = s + (cq - ck_ref[0, 0, :, pl.ds(start, tq)])
        if masked:
            r = lax.broadcasted_iota(jnp.int32, (tq, tq), 0)
            c = lax.broadcasted_iota(jnp.int32, (tq, tq), 1)
            s = jnp.where(c <= r, s, NEG_INF)
        m_prev = m_scr[...]
        m_new = jnp.maximum(m_prev, jnp.max(s, axis=1, keepdims=True))
        alpha = jnp.exp(m_prev - m_new)
        p = jnp.exp(s - m_new)
        l_scr[...] = alpha * l_scr[...] + jnp.sum(p, axis=1, keepdims=True)
        acc_scr[...] = alpha * acc_scr[...] + jnp.dot(p.astype(BF16), v, preferred_element_type=F32)
        m_scr[...] = m_new

    def body(j, carry):
        block(j, False)
        return carry

    lax.fori_loop(0, i, body, 0)
    block(i, True)
    o_ref[0, 0] = (acc_scr[...] / l_scr[...]).astype(o_ref.dtype)


def _fox_prompt(q, k, v, c_col, c_row, tq):
    b, h, t, dh = q.shape
    assert t % tq == 0
    return pl.pallas_call(
        functools.partial(_fox_body, tq=tq),
        grid=(b, h, t // tq),
        in_specs=[pl.BlockSpec((1, 1, tq, dh), lambda bi, hi, i: (bi, hi, i, 0)),
                  pl.BlockSpec((1, 1, t, dh), lambda bi, hi, i: (bi, hi, 0, 0)),
                  pl.BlockSpec((1, 1, t, dh), lambda bi, hi, i: (bi, hi, 0, 0)),
                  pl.BlockSpec((1, 1, tq, 1), lambda bi, hi, i: (bi, hi, i, 0)),
                  pl.BlockSpec((1, 1, 1, t), lambda bi, hi, i: (bi, hi, 0, 0))],
        out_specs=pl.BlockSpec((1, 1, tq, dh), lambda bi, hi, i: (bi, hi, i, 0)),
        out_shape=jax.ShapeDtypeStruct((b, h, t, dh), BF16),
        scratch_shapes=[pltpu.VMEM((tq, 1), F32), pltpu.VMEM((tq, 1), F32), pltpu.VMEM((tq, dh), F32)],
        compiler_params=_cparams(("arbitrary", "arbitrary", "arbitrary")),
        name="fox_prompt",
    )(q, k, v, c_col, c_row)


DEC_PAGES_PER_STEP = 8
DEC_HEADS_PAD = 16


def _fox_decode_body(pt_ref, q_ref, qc_ref, kn_ref, vnc_ref, lfn_ref, *refs, npp):
    del pt_ref
    k_refs = refs[:npp]
    v_refs = refs[npp:2 * npp]
    lf_refs = refs[2 * npp:3 * npp]
    o_ref = refs[3 * npp]
    m_scr, l_scr, acc_scr, c_scr, s_scr, cs_scr = refs[3 * npp + 1:]
    j = pl.program_id(1)
    nj = pl.num_programs(1)

    @pl.when(j == 0)
    def _():
        m_scr[...] = jnp.full_like(m_scr, NEG_INF)
        l_scr[...] = jnp.zeros_like(l_scr)
        acc_scr[...] = jnp.zeros_like(acc_scr)
        c_scr[...] = jnp.zeros_like(c_scr)
        s_scr[...] = jnp.zeros_like(s_scr)

    r = lax.broadcasted_iota(jnp.int32, (PAGE_SIZE, PAGE_SIZE), 0)
    c = lax.broadcasted_iota(jnp.int32, (PAGE_SIZE, PAGE_SIZE), 1)
    tri = (r <= c).astype(BF16)

    carry = c_scr[...]
    for pg in range(npp):
        a1, a2, a3 = _split3(lf_refs[pg][0])
        cs = (jnp.dot(a1, tri, preferred_element_type=F32) + jnp.dot(a2, tri, preferred_element_type=F32)
              + jnp.dot(a3, tri, preferred_element_type=F32)) + carry
        cs_scr[:, pg * PAGE_SIZE:(pg + 1) * PAGE_SIZE] = cs
        carry = cs[:, PAGE_SIZE - 1:PAGE_SIZE]
    c_scr[...] = carry

    for h in range(FOX_H):
        qb = jnp.broadcast_to(qc_ref[0, h], (FOX_DH, PAGE_SIZE))
        for pg in range(npp):
            s_scr[h:h + 1, pg * PAGE_SIZE:(pg + 1) * PAGE_SIZE] = jnp.sum(
                k_refs[pg][0, h] * qb, axis=0, keepdims=True)

    s = s_scr[...] - cs_scr[...]
    m_prev = m_scr[...]
    m_new = jnp.maximum(m_prev, jnp.max(s, axis=1, keepdims=True))
    alpha = jnp.exp(m_prev - m_new)
    p = jnp.exp(s - m_new)
    l_scr[...] = alpha * l_scr[...] + jnp.sum(p, axis=1, keepdims=True)
    m_scr[...] = m_new
    cs_scr[...] = p
    for h in range(FOX_H):
        acc = acc_scr[h] * alpha[h:h + 1, :]
        for pg in range(npp):
            acc = acc + v_refs[pg][0, h] * cs_scr[h:h + 1, pg * PAGE_SIZE:(pg + 1) * PAGE_SIZE]
        acc_scr[h] = acc

    @pl.when(j == nj - 1)
    def _():
        s_new = jnp.sum(q_ref[0] * kn_ref[0], axis=1, keepdims=True) - (c_scr[...] + lfn_ref[0])
        m_last = m_scr[...]
        m_fin = jnp.maximum(m_last, s_new)
        a_fin = jnp.exp(m_last - m_fin)
        p_new = jnp.exp(s_new - m_fin)
        l_fin = a_fin * l_scr[...] + p_new
        o_ref[...] = jnp.zeros_like(o_ref)
        for h in range(FOX_H):
            tot = jnp.sum(acc_scr[h], axis=1, keepdims=True)
            o_ref[0, h] = (a_fin[h:h + 1, :] * tot + p_new[h:h + 1, :] * vnc_ref[0, h]) / l_fin[h:h + 1, :]


def _fox_decode(page_table, q, q_col, k_new, v_new_col, lf_new, cache_kt, cache_vt, cache_lf_t):
    b, n_pages = page_table.shape
    npp = DEC_PAGES_PER_STEP
    hp = DEC_HEADS_PAD
    assert n_pages % npp == 0
    tok = lambda w: pl.BlockSpec((1, hp, w), lambda bi, j, pt: (bi, 0, 0))
    col = pl.BlockSpec((1, hp, FOX_DH, 1), lambda bi, j, pt: (bi, 0, 0, 0))

    def page_spec(shape_tail, pg):
        nd = len(shape_tail)
        return pl.BlockSpec((1,) + shape_tail, lambda bi, j, pt, pg=pg: (pt[bi, j * npp + pg],) + (0,) * nd)

    in_specs = [tok(FOX_DH), col, tok(FOX_DH), col, tok(1)]
    in_specs += [page_spec((FOX_H, FOX_DH, PAGE_SIZE), pg) for pg in range(npp)]
    in_specs += [page_spec((FOX_H, FOX_DH, PAGE_SIZE), pg) for pg in range(npp)]
    in_specs += [page_spec((hp, PAGE_SIZE), pg) for pg in range(npp)]
    grid_spec = pltpu.PrefetchScalarGridSpec(
        num_scalar_prefetch=1,
        grid=(b, n_pages // npp),
        in_specs=in_specs,
        out_specs=col,
        scratch_shapes=[pltpu.VMEM((hp, 1), F32), pltpu.VMEM((hp, 1), F32),
                        pltpu.VMEM((FOX_H, FOX_DH, PAGE_SIZE), F32), pltpu.VMEM((hp, 1), F32),
                        pltpu.VMEM((hp, npp * PAGE_SIZE), F32), pltpu.VMEM((hp, npp * PAGE_SIZE), F32)],
    )
    return pl.pallas_call(
        functools.partial(_fox_decode_body, npp=npp),
        grid_spec=grid_spec,
        out_shape=jax.ShapeDtypeStruct((b, hp, FOX_DH, 1), F32),
        compiler_params=_cparams(("arbitrary", "arbitrary")),
        name="fox_decode",
    )(page_table, q, q_col, k_new, v_new_col, lf_new,
      *([cache_kt] * npp), *([cache_vt] * npp), *([cache_lf_t] * npp))


def _heads(x, b, t):
    return x.reshape(b, t, FOX_H, FOX_DH).transpose(0, 2, 1, 3)


def kernel(x_prompt, x_sample, cache_k, cache_v, cache_logf, cache_mem_k, cache_mem_v, state_ssm_re, state_ssm_im, page_table, mem_prompt, w_in, w_out, ln1_g, ln1_b, ln2_g, ln2_b, w_mem_kv, ssm_a_re, ssm_a_im, ssm_log_dt, ssm_b_re, ssm_b_im, ssm_c_re, ssm_c_im, ssm_d, w_glu, b_glu, w_kvf, b_f, w_r1, b_r1, w_r2, b_r2, w_e_gate, w_e_up, w_e_down):
    bp, tp, _ = x_prompt.shape
    bs = x_sample.shape[0]
    n_p = bp * tp
    n_mem = mem_prompt.shape[1]
    tm = 512

    w_in_bf = w_in.astype(BF16)
    w_out_bf = w_out.astype(BF16)
    w_glu_bf = w_glu.astype(BF16)
    wkv_bf = w_kvf[:, :2 * MAIN_W].astype(BF16)
    wf_bf = jnp.pad(w_kvf[:, 2 * MAIN_W:], ((0, 0), (0, LANES - FOX_H))).astype(BF16)
    bf_pad = jnp.pad(b_f, (0, LANES - FOX_H))[None, :]
    n_e = MOE_GROUPS * MOE_EPG
    wr = jnp.concatenate([w_r1, w_r2.transpose(0, 2, 1, 3).reshape(DEPTH, D_MODEL, n_e)], axis=-1)
    wr_bf = jnp.pad(wr, ((0, 0), (0, 0), (0, LANES - MOE_GROUPS - n_e))).astype(BF16)
    br = jnp.pad(jnp.concatenate([b_r1, b_r2.reshape(DEPTH, n_e)], axis=-1),
                 ((0, 0), (0, LANES - MOE_GROUPS - n_e)))[:, None, :]
    hf = MOE_EPG * MOE_F

    def group_cols(w):
        return w.reshape(DEPTH, MOE_GROUPS, MOE_EPG, D_MODEL, MOE_F).transpose(0, 1, 3, 2, 4).reshape(
            DEPTH, MOE_GROUPS, D_MODEL, hf)

    wgu_bf = jnp.concatenate([group_cols(w_e_gate), group_cols(w_e_up)], axis=-1).astype(BF16)
    wd_bf = w_e_down.reshape(DEPTH, MOE_GROUPS, hf, D_MODEL).astype(BF16)
    sw = _s5_weights(ssm_a_re[0], ssm_a_im[0], ssm_log_dt[0], ssm_b_re[0], ssm_b_im[0], ssm_c_re[0], ssm_c_im[0])

    mem2d = mem_prompt.reshape(bp * n_mem, D_MODEL)
    p_mem = [_proj(mem2d, w_mem_kv[l].astype(BF16), (MEM_W, MEM_W), (F32, F32), tm) for l in range(DEPTH)]
    p_mem_k = jnp.stack([pm[0].reshape(bp, n_mem, MEM_H, MEM_DH) for pm in p_mem])
    p_mem_v = jnp.stack([pm[1].reshape(bp, n_mem, MEM_H, MEM_DH) for pm in p_mem])

    xp = x_prompt.reshape(n_p, D_MODEL)
    xs = x_sample.reshape(bs, D_MODEL)

    up, qmp = _proj(xp, w_in_bf[0], (MAIN_W, MEM_W), (F32, BF16), tm)
    us, qms = _proj(xs, w_in_bf[0], (MAIN_W, MEM_W), (F32, BF16), bs)
    zp, hfin = _s5_prompt(up.reshape(bp, tp, MAIN_W), sw, ssm_d, 2048)
    p_ssm_re = hfin[:, :, 0, :S5_STATE].reshape(1, bp, SSM_G, SSM_P)
    p_ssm_im = hfin[:, :, 0, S5_STATE:].reshape(1, bp, SSM_G, SSM_P)

    def lt_state(s):
        return s.reshape(bs, S5_NLT, S5_STATE).transpose(1, 0, 2)

    h0 = jnp.concatenate([lt_state(state_ssm_re[0]), lt_state(state_ssm_im[0])], axis=-1)
    zs, hs = _s5_step(us, h0, sw, ssm_d)
    s_ssm_re = hs[:, :, :S5_STATE].transpose(1, 0, 2).reshape(1, bs, SSM_G, SSM_P)
    s_ssm_im = hs[:, :, S5_STATE:].transpose(1, 0, 2).reshape(1, bs, SSM_G, SSM_P)

    def mem_kv_bf(x, b):
        return x.reshape(b, n_mem, MEM_W).astype(BF16)

    memp = _mem_attn(qmp.reshape(bp, tp, MEM_W), mem_kv_bf(p_mem_k[0], bp), mem_kv_bf(p_mem_v[0], bp), tm)
    mems = _mem_attn(qms.reshape(bs, 1, MEM_W), mem_kv_bf(cache_mem_k[0], bs), mem_kv_bf(cache_mem_v[0], bs), 1)

    x1p, rp = _mix(xp, zp.reshape(n_p, MAIN_W), memp.reshape(n_p, MEM_W), w_out_bf[0], ln1_g[0:1], ln1_b[0:1],
                   wr_bf[0], br[0], tm, w_glu_bf[0], b_glu[0:1])
    x1s, rs = _mix(xs, zs, mems.reshape(bs, MEM_W), w_out_bf[0], ln1_g[0:1], ln1_b[0:1],
                   wr_bf[0], br[0], bs, w_glu_bf[0], b_glu[0:1])
    x2p, x2s = _moe_layer([x1p, x1s], [rp, rs], wgu_bf[0], wd_bf[0], ln2_g[0:1], ln2_b[0:1])

    kp, vp, kpb, vpb, lfp, cp = _kvf(x2p, wkv_bf, wf_bf, bf_pad, bp, tm)
    ks, vs, _, _, lfs, _ = _kvf(x2s, wkv_bf, wf_bf, bf_pad, 1, bs)
    p_k = kp.reshape(bp, tp, FOX_H, FOX_DH)
    p_v = vp.reshape(bp, tp, FOX_H, FOX_DH)
    p_logf = lfp[:, :FOX_H].reshape(bp, tp, FOX_H)
    s_k = ks.reshape(bs, 1, FOX_H, FOX_DH)
    s_v = vs.reshape(bs, 1, FOX_H, FOX_DH)
    s_logf = lfs[:, :FOX_H].reshape(bs, 1, FOX_H)

    qp, qmp = _proj(x2p, w_in_bf[1], (MAIN_W, MEM_W), (BF16, BF16), tm)
    qs, qms = _proj(x2s, w_in_bf[1], (MAIN_W, MEM_W), (F32, BF16), bs)
    scale = FOX_DH ** -0.5
    c_hm = cp[:, :FOX_H].reshape(bp, tp, FOX_H).transpose(0, 2, 1)
    attn = _fox_prompt(_heads(qp * jnp.asarray(scale, BF16), bp, tp), _heads(kpb, bp, tp), _heads(vpb, bp, tp),
                       c_hm[..., None], c_hm[:, :, None, :], 512)
    attn_p = attn.transpose(0, 2, 1, 3).reshape(n_p, MAIN_W)
    hpad = DEC_HEADS_PAD - FOX_H

    def pad_heads(x):
        return jnp.pad(x.reshape(bs, FOX_H, FOX_DH), ((0, 0), (0, hpad), (0, 0)))

    cache_lf_t = jnp.pad(cache_logf.transpose(0, 2, 1), ((0, 0), (0, hpad), (0, 0)))
    q_dec = pad_heads(qs * scale)
    attn_s = _fox_decode(page_table, q_dec, q_dec[..., None], pad_heads(ks), pad_heads(vs)[..., None],
                         lfs[:, :DEC_HEADS_PAD, None] * (jnp.arange(DEC_HEADS_PAD) < FOX_H)[None, :, None],
                         cache_k.transpose(0, 2, 3, 1), cache_v.transpose(0, 2, 3, 1),
                         cache_lf_t)[:, :FOX_H, :, 0].reshape(bs, MAIN_W)

    memp = _mem_attn(qmp.reshape(bp, tp, MEM_W), mem_kv_bf(p_mem_k[1], bp), mem_kv_bf(p_mem_v[1], bp), tm)
    mems = _mem_attn(qms.reshape(bs, 1, MEM_W), mem_kv_bf(cache_mem_k[1], bs), mem_kv_bf(cache_mem_v[1], bs), 1)
    x1p, rp = _mix(x2p, attn_p, memp.reshape(n_p, MEM_W), w_out_bf[1], ln1_g[1:2], ln1_b[1:2], wr_bf[1], br[1], tm)
    x1s, rs = _mix(x2s, attn_s, mems.reshape(bs, MEM_W), w_out_bf[1], ln1_g[1:2], ln1_b[1:2], wr_bf[1], br[1], bs)
    yp, ys = _moe_layer([x1p, x1s], [rp, rs], wgu_bf[1], wd_bf[1], ln2_g[1:2], ln2_b[1:2])

    return (yp.reshape(bp, tp, D_MODEL), ys.reshape(bs, 1, D_MODEL), p_ssm_re, p_ssm_im, p_k, p_v, p_logf,
            p_mem_k, p_mem_v, s_ssm_re, s_ssm_im, s_k, s_v, s_logf)
```

```python
import functools
import math

import jax
import jax.numpy as jnp
from jax import lax
from jax.experimental import pallas as pl
from jax.experimental.pallas import tpu as pltpu

F32 = jnp.float32
BF16 = jnp.bfloat16

D_MODEL = 1024
DEPTH = 2
MAIN_W = 768
MEM_W = 256
SSM_GC = 16
SSM_G = 48
SSM_P = 64
FOX_DH = 64
FOX_H = 12
MEM_H = 4
MEM_DH = 64
PAGE_SIZE = 128
MOE_GROUPS = 4
MOE_EPG = 4
MOE_F = 256
DN_ALPHA = (2.0 * DEPTH) ** 0.25
LN_EPS = 1e-5
NEG_INF = -1e30

LANES = 128
S5_CHUNK = 8
S5_GPL = LANES // SSM_GC
S5_NLT = MAIN_W // LANES
S5_STATE = S5_GPL * SSM_P
VMEM_LIMIT = 56 * 1024 * 1024


def _cparams(sem):
    return pltpu.CompilerParams(dimension_semantics=sem, vmem_limit_bytes=VMEM_LIMIT)


def _proj_body(x_ref, w_ref, *o_refs, widths):
    y = jnp.dot(x_ref[...].astype(BF16), w_ref[...], preferred_element_type=F32)
    off = 0
    for o_ref, n in zip(o_refs, widths):
        o_ref[...] = y[:, off:off + n].astype(o_ref.dtype)
        off += n


def _proj(x, w_bf, widths, dtypes, tm):
    m, k = x.shape
    n = w_bf.shape[1]
    assert m % tm == 0 and sum(widths) == n
    return pl.pallas_call(
        functools.partial(_proj_body, widths=tuple(widths)),
        grid=(m // tm,),
        in_specs=[pl.BlockSpec((tm, k), lambda i: (i, 0)),
                  pl.BlockSpec((k, n), lambda i: (0, 0))],
        out_specs=[pl.BlockSpec((tm, w), lambda i: (i, 0)) for w in widths],
        out_shape=[jax.ShapeDtypeStruct((m, w), dt) for w, dt in zip(widths, dtypes)],
        compiler_params=_cparams(("arbitrary",)),
        name="proj",
    )(x, w_bf)


def _gelu(x):
    c = math.sqrt(2.0 / math.pi)
    return 0.5 * x * (1.0 + jnp.tanh(c * (x + 0.044715 * (x * x * x))))


def _s5_weights(a_re, a_im, log_dt, b_re, b_im, c_re, c_im):
    f32 = F32
    L = S5_CHUNK
    a_re, a_im = a_re.astype(f32), a_im.astype(f32)
    dt = jnp.exp(log_dt.astype(f32))[:, None]
    mag = jnp.exp(a_re * dt)
    lb_re = mag * jnp.cos(a_im * dt)
    lb_im = mag * jnp.sin(a_im * dt)
    den = a_re * a_re + a_im * a_im
    zr = ((lb_re - 1.0) * a_re + lb_im * a_im) / den
    zi = (lb_im * a_re - (lb_re - 1.0) * a_im) / den
    b_re, b_im = b_re.astype(f32), b_im.astype(f32)
    bb_re = zr[..., None] * b_re - zi[..., None] * b_im
    bb_im = zr[..., None] * b_im + zi[..., None] * b_re
    ks = jnp.arange(L + 1, dtype=f32)[:, None, None]
    mag_k = jnp.exp(a_re * dt * ks)
    pk_re = mag_k * jnp.cos(a_im * dt * ks)
    pk_im = mag_k * jnp.sin(a_im * dt * ks)
    c_re, c_im = c_re.astype(f32), c_im.astype(f32)

    eye = jnp.eye(S5_GPL, dtype=f32)

    pb_re = pk_re[..., None] * bb_re - pk_im[..., None] * bb_im
    pb_im = pk_re[..., None] * bb_im + pk_im[..., None] * bb_re

    def lt_w1(pb):
        w = pb[L - 1 - jnp.arange(L)]
        w = w.transpose(1, 0, 3, 2).reshape(S5_NLT, S5_GPL, L, SSM_GC, SSM_P)
        w = w.transpose(0, 2, 1, 3, 4)[:, :, :, :, None, :] * eye[None, None, :, None, :, None]
        return w.reshape(S5_NLT, L * LANES, S5_STATE)

    w1 = jnp.concatenate([lt_w1(pb_re), lt_w1(pb_im)], axis=-1)

    kk = (jnp.sum(pb_re[:L].transpose(0, 1, 3, 2)[:, :, :, None, :] * c_re[None, :, None, :, :], axis=-1)
          - jnp.sum(pb_im[:L].transpose(0, 1, 3, 2)[:, :, :, None, :] * c_im[None, :, None, :, :], axis=-1))
    ti = jnp.arange(L)[:, None]
    to = jnp.arange(L)[None, :]
    lag = jnp.clip(to - ti, 0, L - 1)
    mm = jnp.where((to >= ti)[:, :, None, None, None], kk[lag], 0.0)
    mm = mm.transpose(2, 0, 3, 1, 4).reshape(S5_NLT, S5_GPL, L, SSM_GC, L, SSM_GC)
    mm = mm.transpose(0, 2, 1, 3, 4, 5)[:, :, :, :, :, None, :] * eye[None, None, :, None, None, :, None]
    mm = mm.reshape(S5_NLT, L * LANES, L * LANES)

    def lt_w2(x):
        w = x.transpose(1, 3, 0, 2).reshape(S5_NLT, S5_GPL, SSM_P, L, SSM_GC)
        w = w[:, :, :, :, None, :] * eye[None, :, None, None, :, None]
        return w.reshape(S5_NLT, S5_STATE, L * LANES)

    ca_re = c_re[None] * pk_re[1:, :, None, :] - c_im[None] * pk_im[1:, :, None, :]
    ca_im = c_re[None] * pk_im[1:, :, None, :] + c_im[None] * pk_re[1:, :, None, :]
    w2 = jnp.concatenate([lt_w2(ca_re), lt_w2(-ca_im)], axis=1)

    def lt_vec(x):
        return x.reshape(S5_NLT, 1, S5_STATE)

    a_chunk = jnp.concatenate([lt_vec(pk_re[L]), lt_vec(pk_im[L])], axis=-1)
    a_one = jnp.concatenate([lt_vec(pk_re[1]), lt_vec(pk_im[1])], axis=-1)

    def lt_c0(x):
        w = x.transpose(0, 2, 1).reshape(S5_NLT, S5_GPL, SSM_P, SSM_GC)
        w = w[:, :, :, None, :] * eye[None, :, None, :, None]
        return w.reshape(S5_NLT, S5_STATE, LANES)

    c0 = jnp.concatenate([lt_c0(c_re), lt_c0(-c_im)], axis=1)
    bb1 = w1[:, (L - 1) * LANES:, :]
    return dict(w1m=jnp.concatenate([w1, mm], axis=-1).astype(BF16), w2=w2.astype(BF16),
                a_chunk=a_chunk, a_one=a_one, c0=c0.astype(BF16), bb1=bb1.astype(BF16))


def _s5_body(u_ref, w1m_ref, w2_ref, ach_ref, d_ref, z_ref, hfin_ref, s_scr, hp_scr, h_scr, *, tm):
    t = pl.program_id(2)
    nc = tm // S5_CHUNK
    ns = S5_STATE

    @pl.when(t == 0)
    def _():
        h_scr[...] = jnp.zeros_like(h_scr)

    parts = [u_ref[0, pl.ds(k, nc, stride=S5_CHUNK), :] for k in range(S5_CHUNK)]
    a = jnp.concatenate(parts, axis=1).astype(BF16)
    sy = jnp.dot(a, w1m_ref[0], preferred_element_type=F32)
    s_scr[...] = sy[:, :2 * ns]
    ar = ach_ref[0, :, :ns]
    ai = ach_ref[0, :, ns:]

    def step(j, carry):
        hr, hi = carry
        hp_scr[pl.ds(j, 1), :ns] = hr
        hp_scr[pl.ds(j, 1), ns:] = hi
        sr = s_scr[pl.ds(j, 1), :ns]
        si = s_scr[pl.ds(j, 1), ns:]
        return ar * hr - ai * hi + sr, ar * hi + ai * hr + si

    hr, hi = lax.fori_loop(0, nc, step, (h_scr[:, :ns], h_scr[:, ns:]), unroll=8)
    h_scr[:, :ns] = hr
    h_scr[:, ns:] = hi
    hfin_ref[0, 0] = h_scr[...]

    yi = jnp.dot(hp_scr[...].astype(BF16), w2_ref[0], preferred_element_type=F32)
    d = d_ref[...]
    for k in range(S5_CHUNK):
        yk = sy[:, 2 * ns + k * LANES:2 * ns + (k + 1) * LANES] + yi[:, k * LANES:(k + 1) * LANES] + d * parts[k]
        z_ref[0, pl.ds(k, nc, stride=S5_CHUNK), :] = _gelu(yk)


def _s5_prompt(u, sw, d_skip, tm):
    b, t, _ = u.shape
    assert t % tm == 0 and tm % (8 * S5_CHUNK) == 0
    nc = tm // S5_CHUNK
    ns2 = 2 * S5_STATE
    kw = S5_CHUNK * LANES
    return pl.pallas_call(
        functools.partial(_s5_body, tm=tm),
        grid=(S5_NLT, b, t // tm),
        in_specs=[pl.BlockSpec((1, tm, LANES), lambda l, bb, tt: (bb, tt, l)),
                  pl.BlockSpec((1, kw, ns2 + kw), lambda l, bb, tt: (l, 0, 0)),
                  pl.BlockSpec((1, ns2, kw), lambda l, bb, tt: (l, 0, 0)),
                  pl.BlockSpec((1, 1, ns2), lambda l, bb, tt: (l, 0, 0)),
                  pl.BlockSpec((1, LANES), lambda l, bb, tt: (0, l))],
        out_specs=[pl.BlockSpec((1, tm, LANES), lambda l, bb, tt: (bb, tt, l)),
                   pl.BlockSpec((1, 1, 1, ns2), lambda l, bb, tt: (bb, l, 0, 0))],
        out_shape=[jax.ShapeDtypeStruct(u.shape, F32),
                   jax.ShapeDtypeStruct((b, S5_NLT, 1, ns2), F32)],
        scratch_shapes=[pltpu.VMEM((nc, ns2), F32), pltpu.VMEM((nc, ns2), F32), pltpu.VMEM((1, ns2), F32)],
        compiler_params=_cparams(("arbitrary", "arbitrary", "arbitrary")),
        name="s5_prompt",
    )(u, sw["w1m"], sw["w2"], sw["a_chunk"], d_skip)


def _s5_step_body(u_ref, h0_ref, bb1_ref, c0_ref, a1_ref, d_ref, z_ref, h_ref):
    ns = S5_STATE
    for l in range(S5_NLT):
        u = u_ref[:, l * LANES:(l + 1) * LANES]
        bu = jnp.dot(u.astype(BF16), bb1_ref[l], preferred_element_type=F32)
        ar = a1_ref[l, :, :ns]
        ai = a1_ref[l, :, ns:]
        h0r = h0_ref[l, :, :ns]
        h0i = h0_ref[l, :, ns:]
        hr = ar * h0r - ai * h0i + bu[:, :ns]
        hi = ar * h0i + ai * h0r + bu[:, ns:]
        h_ref[l, :, :ns] = hr
        h_ref[l, :, ns:] = hi
        h = jnp.concatenate([hr, hi], axis=1).astype(BF16)
        y = jnp.dot(h, c0_ref[l], preferred_element_type=F32) + d_ref[:, l * LANES:(l + 1) * LANES] * u
        z_ref[:, l * LANES:(l + 1) * LANES] = _gelu(y)


def _s5_step(u, h0, sw, d_skip):
    n = u.shape[0]
    return pl.pallas_call(
        _s5_step_body,
        out_shape=[jax.ShapeDtypeStruct((n, MAIN_W), F32),
                   jax.ShapeDtypeStruct((S5_NLT, n, 2 * S5_STATE), F32)],
        compiler_params=pltpu.CompilerParams(vmem_limit_bytes=VMEM_LIMIT),
        name="s5_step",
    )(u, h0, sw["bb1"], sw["c0"], sw["a_one"], d_skip)


def _mem_attn_body(q_ref, k_ref, v_ref, o_ref):
    q = q_ref[0]
    tm = q.shape[0]
    rows = max(tm, 8)
    outs = []
    for h in range(MEM_H):
        sl = slice(h * MEM_DH, (h + 1) * MEM_DH)
        qh = q[:, sl]
        if tm < rows:
            qh = jnp.broadcast_to(qh, (rows, MEM_DH))
        s = lax.dot_general(qh, k_ref[0, :, sl], (((1,), (1,)), ((), ())),
                            preferred_element_type=F32) * (MEM_DH ** -0.5)
        m = jnp.max(s, axis=1, keepdims=True)
        e = jnp.exp(s - m)
        p = e / jnp.sum(e, axis=1, keepdims=True)
        o = jnp.dot(p.astype(BF16), v_ref[0, :, sl], preferred_element_type=F32)
        outs.append(o[:tm])
    o_ref[0] = jnp.concatenate(outs, axis=1).astype(o_ref.dtype)


def _mem_attn(q, k, v, tm):
    b, t, w = q.shape
    nm = k.shape[1]
    return pl.pallas_call(
        _mem_attn_body,
        grid=(b, t // tm),
        in_specs=[pl.BlockSpec((1, tm, w), lambda i, j: (i, j, 0)),
                  pl.BlockSpec((1, nm, w), lambda i, j: (i, 0, 0)),
                  pl.BlockSpec((1, nm, w), lambda i, j: (i, 0, 0))],
        out_specs=pl.BlockSpec((1, tm, w), lambda i, j: (i, j, 0)),
        out_shape=jax.ShapeDtypeStruct((b, t, w), BF16),
        compiler_params=_cparams(("arbitrary", "arbitrary")),
        name="mem_attn",
    )(q, k, v)


def _layer_norm(x, g, b):
    mu = jnp.mean(x, axis=-1, keepdims=True)
    xc = x - mu
    var = jnp.mean(xc * xc, axis=-1, keepdims=True)
    return xc * lax.rsqrt(var + LN_EPS) * g + b


def _route(logits):
    lane = lax.broadcasted_iota(jnp.int32, logits.shape, 1).astype(F32)
    lo = jnp.float32(jnp.finfo(jnp.float32).min)
    none = jnp.float32(LANES)
    is_g = lane < MOE_GROUPS
    gl = jnp.where(is_g, logits, lo)
    gm = jnp.max(gl, axis=1, keepdims=True)
    gsum = jnp.sum(jnp.where(is_g, jnp.exp(gl - gm), 0.0), axis=1, keepdims=True)
    p_g = 1.0 / gsum
    g_idx = jnp.min(jnp.where(is_g & (gl == gm), lane, none), axis=1, keepdims=True)
    e_lo = MOE_GROUPS + MOE_EPG * g_idx
    is_e = (lane >= e_lo) & (lane < e_lo + MOE_EPG)
    el = jnp.where(is_e, logits, lo)
    v1 = jnp.max(el, axis=1, keepdims=True)
    i1 = jnp.min(jnp.where(is_e & (el == v1), lane, none), axis=1, keepdims=True)
    is_e2 = is_e & (lane != i1)
    el2 = jnp.where(is_e2, logits, lo)
    v2 = jnp.max(el2, axis=1, keepdims=True)
    i2 = jnp.min(jnp.where(is_e2 & (el2 == v2), lane, none), axis=1, keepdims=True)
    r = jnp.exp(v2 - v1)
    w1 = p_g / (1.0 + r)
    w2 = p_g * r / (1.0 + r)
    out = jnp.where(lane == i1, w1, 0.0) + jnp.where(lane == i2, w2, 0.0)
    return out + jnp.where(lane == 0.0, g_idx, 0.0)


def _mix_body(*refs, glu):
    if glu:
        (x_ref, main_ref, mem_ref, wglu_ref, bglu_ref, wout_ref, g_ref, b_ref, wr_ref, br_ref,
         x1_ref, route_ref) = refs
        z = main_ref[...]
        gate = jnp.dot(z.astype(BF16), wglu_ref[...], preferred_element_type=F32) + bglu_ref[...]
        main = (z * jax.nn.sigmoid(gate)).astype(BF16)
    else:
        x_ref, main_ref, mem_ref, wout_ref, g_ref, b_ref, wr_ref, br_ref, x1_ref, route_ref = refs
        main = main_ref[...].astype(BF16)
    mixed = (jnp.dot(main, wout_ref[:MAIN_W, :], preferred_element_type=F32)
             + jnp.dot(mem_ref[...], wout_ref[MAIN_W:, :], preferred_element_type=F32))
    x1 = _layer_norm(DN_ALPHA * x_ref[...] + mixed, g_ref[...], b_ref[...])
    x1_ref[...] = x1
    logits = jnp.dot(x1.astype(BF16), wr_ref[...], preferred_element_type=F32) + br_ref[...]
    route_ref[...] = _route(logits)


def _mix(x, main, mem, w_out_bf, ln_g, ln_b, wr_bf, br, tm, w_glu_bf=None, b_glu=None):
    n = x.shape[0]
    glu = w_glu_bf is not None
    row = lambda w: pl.BlockSpec((tm, w), lambda i: (i, 0))
    full = lambda a: pl.BlockSpec(a.shape, lambda i: (0,) * a.ndim)
    args = [x, main, mem]
    specs = [row(D_MODEL), row(MAIN_W), row(MEM_W)]
    if glu:
        args += [w_glu_bf, b_glu]
        specs += [full(w_glu_bf), full(b_glu)]
    tail = [w_out_bf, ln_g, ln_b, wr_bf, br]
    args += tail
    specs += [full(a) for a in tail]
    return pl.pallas_call(
        functools.partial(_mix_body, glu=glu),
        grid=(n // tm,),
        in_specs=specs,
        out_specs=[row(D_MODEL), row(LANES)],
        out_shape=[jax.ShapeDtypeStruct((n, D_MODEL), F32), jax.ShapeDtypeStruct((n, LANES), F32)],
        compiler_params=_cparams(("arbitrary",)),
        name="mix_glu" if glu else "mix",
    )(*args)


def _moe_body(tg_ref, x_ref, cw_ref, wgu_ref, wd_ref, g_ref, b_ref, o_ref):
    del tg_ref
    x = x_ref[...]
    hf = MOE_EPG * MOE_F
    gu = jnp.dot(x.astype(BF16), wgu_ref[0], preferred_element_type=F32)
    h = jax.nn.silu(gu[:, :hf]) * gu[:, hf:]
    cw = cw_ref[...]
    hs = [h[:, e * MOE_F:(e + 1) * MOE_F] * cw[:, e:e + 1] for e in range(MOE_EPG)]
    hb = jnp.concatenate(hs, axis=1).astype(BF16)
    y = jnp.dot(hb, wd_ref[0], preferred_element_type=F32)
    o_ref[...] = _layer_norm(DN_ALPHA * x + y, g_ref[...], b_ref[...])


def _moe(tile_group, xs, cws, wgu_bf, wd_bf, ln_g, ln_b, tm):
    p = xs.shape[0]
    hf = MOE_EPG * MOE_F
    grid_spec = pltpu.PrefetchScalarGridSpec(
        num_scalar_prefetch=1,
        grid=(p // tm,),
        in_specs=[pl.BlockSpec((tm, D_MODEL), lambda i, tg: (i, 0)),
                  pl.BlockSpec((tm, MOE_EPG), lambda i, tg: (i, 0)),
                  pl.BlockSpec((1, D_MODEL, 2 * hf), lambda i, tg: (tg[i], 0, 0)),
                  pl.BlockSpec((1, hf, D_MODEL), lambda i, tg: (tg[i], 0, 0)),
                  pl.BlockSpec((1, D_MODEL), lambda i, tg: (0, 0)),
                  pl.BlockSpec((1, D_MODEL), lambda i, tg: (0, 0))],
        out_specs=pl.BlockSpec((tm, D_MODEL), lambda i, tg: (i, 0)),
    )
    return pl.pallas_call(
        _moe_body,
        grid_spec=grid_spec,
        out_shape=jax.ShapeDtypeStruct((p, D_MODEL), F32),
        compiler_params=_cparams(("arbitrary",)),
        name="moe",
    )(tile_group, xs, cws, wgu_bf, wd_bf, ln_g, ln_b)


MOE_TM = 256


def _moe_layer(x1_parts, route_parts, wgu_bf, wd_bf, ln_g, ln_b):
    x1 = jnp.concatenate(x1_parts, axis=0)
    route = jnp.concatenate(route_parts, axis=0)
    n = x1.shape[0]
    tm = MOE_TM
    g_idx = route[:, 0].astype(jnp.int32)
    cw16 = route[:, MOE_GROUPS:MOE_GROUPS + MOE_GROUPS * MOE_EPG].reshape(n, MOE_GROUPS, MOE_EPG)
    cw = jnp.take_along_axis(cw16, g_idx[:, None, None], axis=1)[:, 0, :]
    onehot = (g_idx[:, None] == jnp.arange(MOE_GROUPS)[None, :]).astype(jnp.int32)
    counts = jnp.sum(onehot, axis=0)
    rank = jnp.sum((jnp.cumsum(onehot, axis=0) - onehot) * onehot, axis=1)
    tiles = (counts + tm - 1) // tm
    tile_start = jnp.cumsum(tiles) - tiles
    pos = tile_start[g_idx] * tm + rank
    n_tiles = (n + tm - 1) // tm + MOE_GROUPS
    p = n_tiles * tm
    src = jnp.full((p,), n, jnp.int32).at[pos].set(jnp.arange(n, dtype=jnp.int32))
    tile_end = jnp.cumsum(tiles)
    tile_group = jnp.minimum(jnp.sum((jnp.arange(n_tiles)[:, None] >= tile_end[None, :]).astype(jnp.int32), axis=1),
                             MOE_GROUPS - 1).astype(jnp.int32)
    xs = x1[jnp.minimum(src, n - 1)]
    cws = jnp.concatenate([cw, jnp.zeros((1, MOE_EPG), F32)], axis=0)[src]
    ys = _moe(tile_group, xs, cws, wgu_bf, wd_bf, ln_g, ln_b, tm)
    y = ys[pos]
    outs, off = [], 0
    for part in x1_parts:
        outs.append(y[off:off + part.shape[0]])
        off += part.shape[0]
    return outs


def _log_sigmoid(x):
    return -(jnp.maximum(-x, 0.0) + jnp.log1p(jnp.exp(-jnp.abs(x))))


def _split3(x):
    a1 = x.astype(BF16)
    r1 = x - a1.astype(F32)
    a2 = r1.astype(BF16)
    a3 = (r1 - a2.astype(F32)).astype(BF16)
    return a1, a2, a3


def _kvf_body(x_ref, wkv_ref, wf_ref, bf_ref, k_ref, v_ref, kb_ref, vb_ref, lf_ref, c_ref, carry):
    t = pl.program_id(1)

    @pl.when(t == 0)
    def _():
        carry[...] = jnp.zeros_like(carry)

    xb = x_ref[...].astype(BF16)
    tm = xb.shape[0]
    kv = jnp.dot(xb, wkv_ref[...], preferred_element_type=F32)
    k = kv[:, :MAIN_W]
    v = kv[:, MAIN_W:]
    k_ref[...] = k
    v_ref[...] = v
    kb_ref[...] = k.astype(BF16)
    vb_ref[...] = v.astype(BF16)
    lf = _log_sigmoid(jnp.dot(xb, wf_ref[...], preferred_element_type=F32) + bf_ref[...])
    lf_ref[...] = lf
    r = lax.broadcasted_iota(jnp.int32, (tm, tm), 0)
    c = lax.broadcasted_iota(jnp.int32, (tm, tm), 1)
    tri = (c <= r).astype(BF16)
    a1, a2, a3 = _split3(lf)
    cs = (jnp.dot(tri, a1, preferred_element_type=F32) + jnp.dot(tri, a2, preferred_element_type=F32)
          + jnp.dot(tri, a3, preferred_element_type=F32)) + carry[...]
    c_ref[...] = cs
    carry[...] = cs[tm - 1:tm, :]


def _kvf(x, wkv_bf, wf_bf, bf_pad, nb, tm):
    n = x.shape[0]
    t = n // nb
    assert t % tm == 0
    steps = t // tm
    row = lambda w: pl.BlockSpec((tm, w), lambda b, i: (b * steps + i, 0))
    full = lambda a: pl.BlockSpec(a.shape, lambda b, i: (0,) * a.ndim)
    return pl.pallas_call(
        _kvf_body,
        grid=(nb, steps),
        in_specs=[row(D_MODEL), full(wkv_bf), full(wf_bf), full(bf_pad)],
        out_specs=[row(MAIN_W), row(MAIN_W), row(MAIN_W), row(MAIN_W), row(LANES), row(LANES)],
        out_shape=[jax.ShapeDtypeStruct((n, MAIN_W), F32), jax.ShapeDtypeStruct((n, MAIN_W), F32),
                   jax.ShapeDtypeStruct((n, MAIN_W), BF16), jax.ShapeDtypeStruct((n, MAIN_W), BF16),
                   jax.ShapeDtypeStruct((n, LANES), F32), jax.ShapeDtypeStruct((n, LANES), F32)],
        scratch_shapes=[pltpu.VMEM((1, LANES), F32)],
        compiler_params=_cparams(("arbitrary", "arbitrary")),
        name="kvf",
    )(x, wkv_bf, wf_bf, bf_pad)


FOX_ROW_SPLITS = 2


def _fox_body(q_ref, k_ref, v_ref, ck_ref, o_ref, m_scr, l_scr, acc_scr, *, tq):
    i = pl.program_id(2)
    rq = tq // FOX_ROW_SPLITS
    m_scr[...] = jnp.full_like(m_scr, NEG_INF)
    l_scr[...] = jnp.zeros_like(l_scr)
    acc_scr[...] = jnp.zeros_like(acc_scr)

    def block(j, masked):
        start = pl.multiple_of(j * tq, tq)
        for part in range(FOX_ROW_SPLITS):
            rows = slice(part * rq, (part + 1) * rq)
            nk = (part + 1) * rq if masked else tq
            q = q_ref[0, 0, rows, :]
            k = k_ref[0, 0, pl.ds(start, nk), :]
            v = v_ref[0, 0, pl.ds(start, nk), :]
            s = lax.dot_general(q, k, (((1,), (1,)), ((), ())), preferred_element_type=F32)
            s = s - ck_ref[0, 0, :, pl.ds(start, nk)]
            if masked:
                r = lax.broadcasted_iota(jnp.int32, (rq, nk), 0) + part * rq
                c = lax.broadcasted_iota(jnp.int32, (rq, nk), 1)
                s = jnp.where(c <= r, s, NEG_INF)
            m_prev = m_scr[rows, :]
            m_new = jnp.maximum(m_prev, jnp.max(s, axis=1, keepdims=True))
            alpha = jnp.exp(m_prev - m_new)
            p = jnp.exp(s - m_new)
            l_scr[rows, :] = alpha * l_scr[rows, :] + jnp.sum(p, axis=1, keepdims=True)
            acc_scr[rows, :] = alpha * acc_scr[rows, :] + jnp.dot(p.astype(BF16), v, preferred_element_type=F32)
            m_scr[rows, :] = m_new

    def body(j, carry):
        block(j, False)
        return carry

    lax.fori_loop(0, i, body, 0)
    block(i, True)
    o_ref[0, 0] = (acc_scr[...] / l_scr[...]).astype(o_ref.dtype)


def _fox_prompt(q, k, v, c_row, tq):
    b, h, t, dh = q.shape
    assert t % tq == 0 and tq % (FOX_ROW_SPLITS * LANES) == 0
    return pl.pallas_call(
        functools.partial(_fox_body, tq=tq),
        grid=(b, h, t // tq),
        in_specs=[pl.BlockSpec((1, 1, tq, dh), lambda bi, hi, i: (bi, hi, i, 0)),
                  pl.BlockSpec((1, 1, t, dh), lambda bi, hi, i: (bi, hi, 0, 0)),
                  pl.BlockSpec((1, 1, t, dh), lambda bi, hi, i: (bi, hi, 0, 0)),
                  pl.BlockSpec((1, 1, 1, t), lambda bi, hi, i: (bi, hi, 0, 0))],
        out_specs=pl.BlockSpec((1, 1, tq, dh), lambda bi, hi, i: (bi, hi, i, 0)),
        out_shape=jax.ShapeDtypeStruct((b, h, t, dh), BF16),
        scratch_shapes=[pltpu.VMEM((tq, 1), F32), pltpu.VMEM((tq, 1), F32), pltpu.VMEM((tq, dh), F32)],
        compiler_params=_cparams(("arbitrary", "arbitrary", "arbitrary")),
        name="fox_prompt",
    )(q, k, v, c_row)


DEC_PAGES_PER_STEP = 8
DEC_HEADS_PAD = 16


def _fox_decode_body(pt_ref, q_ref, qc_ref, kn_ref, vnc_ref, lfn_ref, *refs, npp):
    del pt_ref
    k_refs = refs[:npp]
    v_refs = refs[npp:2 * npp]
    lf_refs = refs[2 * npp:3 * npp]
    o_ref = refs[3 * npp]
    m_scr, l_scr, acc_scr, c_scr, s_scr, cs_scr = refs[3 * npp + 1:]
    j = pl.program_id(1)
    nj = pl.num_programs(1)

    @pl.when(j == 0)
    def _():
        m_scr[...] = jnp.full_like(m_scr, NEG_INF)
        l_scr[...] = jnp.zeros_like(l_scr)
        acc_scr[...] = jnp.zeros_like(acc_scr)
        c_scr[...] = jnp.zeros_like(c_scr)
        s_scr[...] = jnp.zeros_like(s_scr)

    r = lax.broadcasted_iota(jnp.int32, (PAGE_SIZE, PAGE_SIZE), 0)
    c = lax.broadcasted_iota(jnp.int32, (PAGE_SIZE, PAGE_SIZE), 1)
    tri = (r <= c).astype(BF16)

    carry = c_scr[...]
    for pg in range(npp):
        a1, a2, a3 = _split3(lf_refs[pg][0])
        cs = (jnp.dot(a1, tri, preferred_element_type=F32) + jnp.dot(a2, tri, preferred_element_type=F32)
              + jnp.dot(a3, tri, preferred_element_type=F32)) + carry
        cs_scr[:, pg * PAGE_SIZE:(pg + 1) * PAGE_SIZE] = cs
        carry = cs[:, PAGE_SIZE - 1:PAGE_SIZE]
    c_scr[...] = carry

    for h in range(FOX_H):
        qb = jnp.broadcast_to(qc_ref[0, h], (FOX_DH, PAGE_SIZE))
        for pg in range(npp):
            s_scr[h:h + 1, pg * PAGE_SIZE:(pg + 1) * PAGE_SIZE] = jnp.sum(
                k_refs[pg][0, h] * qb, axis=0, keepdims=True)

    s = s_scr[...] - cs_scr[...]
    m_prev = m_scr[...]
    m_new = jnp.maximum(m_prev, jnp.max(s, axis=1, keepdims=True))
    alpha = jnp.exp(m_prev - m_new)
    p = jnp.exp(s - m_new)
    l_scr[...] = alpha * l_scr[...] + jnp.sum(p, axis=1, keepdims=True)
    m_scr[...] = m_new
    cs_scr[...] = p
    for h in range(FOX_H):
        acc = acc_scr[h] * alpha[h:h + 1, :]
        for pg in range(npp):
            acc = acc + v_refs[pg][0, h] * cs_scr[h:h + 1, pg * PAGE_SIZE:(pg + 1) * PAGE_SIZE]
        acc_scr[h] = acc

    @pl.when(j == nj - 1)
    def _():
        s_new = jnp.sum(q_ref[0] * kn_ref[0], axis=1, keepdims=True) - (c_scr[...] + lfn_ref[0])
        m_last = m_scr[...]
        m_fin = jnp.maximum(m_last, s_new)
        a_fin = jnp.exp(m_last - m_fin)
        p_new = jnp.exp(s_new - m_fin)
        l_fin = a_fin * l_scr[...] + p_new
        o_ref[...] = jnp.zeros_like(o_ref)
        for h in range(FOX_H):
            tot = jnp.sum(acc_scr[h], axis=1, keepdims=True)
            o_ref[0, h] = (a_fin[h:h + 1, :] * tot + p_new[h:h + 1, :] * vnc_ref[0, h]) / l_fin[h:h + 1, :]


def _fox_decode(page_table, q, q_col, k_new, v_new_col, lf_new, cache_kt, cache_vt, cache_lf_t):
    b, n_pages = page_table.shape
    npp = DEC_PAGES_PER_STEP
    hp = DEC_HEADS_PAD
    assert n_pages % npp == 0
    tok = lambda w: pl.BlockSpec((1, hp, w), lambda bi, j, pt: (bi, 0, 0))
    col = pl.BlockSpec((1, hp, FOX_DH, 1), lambda bi, j, pt: (bi, 0, 0, 0))

    def page_spec(shape_tail, pg):
        nd = len(shape_tail)
        return pl.BlockSpec((1,) + shape_tail, lambda bi, j, pt, pg=pg: (pt[bi, j * npp + pg],) + (0,) * nd)

    in_specs = [tok(FOX_DH), col, tok(FOX_DH), col, tok(1)]
    in_specs += [page_spec((FOX_H, FOX_DH, PAGE_SIZE), pg) for pg in range(npp)]
    in_specs += [page_spec((FOX_H, FOX_DH, PAGE_SIZE), pg) for pg in range(npp)]
    in_specs += [page_spec((hp, PAGE_SIZE), pg) for pg in range(npp)]
    grid_spec = pltpu.PrefetchScalarGridSpec(
        num_scalar_prefetch=1,
        grid=(b, n_pages // npp),
        in_specs=in_specs,
        out_specs=col,
        scratch_shapes=[pltpu.VMEM((hp, 1), F32), pltpu.VMEM((hp, 1), F32),
                        pltpu.VMEM((FOX_H, FOX_DH, PAGE_SIZE), F32), pltpu.VMEM((hp, 1), F32),
                        pltpu.VMEM((hp, npp * PAGE_SIZE), F32), pltpu.VMEM((hp, npp * PAGE_SIZE), F32)],
    )
    return pl.pallas_call(
        functools.partial(_fox_decode_body, npp=npp),
        grid_spec=grid_spec,
        out_shape=jax.ShapeDtypeStruct((b, hp, FOX_DH, 1), F32),
        compiler_params=_cparams(("arbitrary", "arbitrary")),
        name="fox_decode",
    )(page_table, q, q_col, k_new, v_new_col, lf_new,
      *([cache_kt] * npp), *([cache_vt] * npp), *([cache_lf_t] * npp))


def _heads(x, b, t):
    return x.reshape(b, t, FOX_H, FOX_DH).transpose(0, 2, 1, 3)


def kernel(x_prompt, x_sample, cache_k, cache_v, cache_logf, cache_mem_k, cache_mem_v, state_ssm_re, state_ssm_im, page_table, mem_prompt, w_in, w_out, ln1_g, ln1_b, ln2_g, ln2_b, w_mem_kv, ssm_a_re, ssm_a_im, ssm_log_dt, ssm_b_re, ssm_b_im, ssm_c_re, ssm_c_im, ssm_d, w_glu, b_glu, w_kvf, b_f, w_r1, b_r1, w_r2, b_r2, w_e_gate, w_e_up, w_e_down):
    bp, tp, _ = x_prompt.shape
    bs = x_sample.shape[0]
    n_p = bp * tp
    n_mem = mem_prompt.shape[1]
    tm = 512

    w_in_bf = w_in.astype(BF16)
    w_out_bf = w_out.astype(BF16)
    w_glu_bf = w_glu.astype(BF16)
    wkv_bf = w_kvf[:, :2 * MAIN_W].astype(BF16)
    wf_bf = jnp.pad(w_kvf[:, 2 * MAIN_W:], ((0, 0), (0, LANES - FOX_H))).astype(BF16)
    bf_pad = jnp.pad(b_f, (0, LANES - FOX_H))[None, :]
    n_e = MOE_GROUPS * MOE_EPG
    wr = jnp.concatenate([w_r1, w_r2.transpose(0, 2, 1, 3).reshape(DEPTH, D_MODEL, n_e)], axis=-1)
    wr_bf = jnp.pad(wr, ((0, 0), (0, 0), (0, LANES - MOE_GROUPS - n_e))).astype(BF16)
    br = jnp.pad(jnp.concatenate([b_r1, b_r2.reshape(DEPTH, n_e)], axis=-1),
                 ((0, 0), (0, LANES - MOE_GROUPS - n_e)))[:, None, :]
    hf = MOE_EPG * MOE_F

    def group_cols(w):
        return w.reshape(DEPTH, MOE_GROUPS, MOE_EPG, D_MODEL, MOE_F).transpose(0, 1, 3, 2, 4).reshape(
            DEPTH, MOE_GROUPS, D_MODEL, hf)

    wgu_bf = jnp.concatenate([group_cols(w_e_gate), group_cols(w_e_up)], axis=-1).astype(BF16)
    wd_bf = w_e_down.reshape(DEPTH, MOE_GROUPS, hf, D_MODEL).astype(BF16)
    sw = _s5_weights(ssm_a_re[0], ssm_a_im[0], ssm_log_dt[0], ssm_b_re[0], ssm_b_im[0], ssm_c_re[0], ssm_c_im[0])

    mem2d = mem_prompt.reshape(bp * n_mem, D_MODEL)
    p_mem = [_proj(mem2d, w_mem_kv[l].astype(BF16), (MEM_W, MEM_W), (F32, F32), tm) for l in range(DEPTH)]
    p_mem_k = jnp.stack([pm[0].reshape(bp, n_mem, MEM_H, MEM_DH) for pm in p_mem])
    p_mem_v = jnp.stack([pm[1].reshape(bp, n_mem, MEM_H, MEM_DH) for pm in p_mem])

    xp = x_prompt.reshape(n_p, D_MODEL)
    xs = x_sample.reshape(bs, D_MODEL)

    up, qmp = _proj(xp, w_in_bf[0], (MAIN_W, MEM_W), (F32, BF16), tm)
    us, qms = _proj(xs, w_in_bf[0], (MAIN_W, MEM_W), (F32, BF16), bs)
    zp, hfin = _s5_prompt(up.reshape(bp, tp, MAIN_W), sw, ssm_d, 2048)
    p_ssm_re = hfin[:, :, 0, :S5_STATE].reshape(1, bp, SSM_G, SSM_P)
    p_ssm_im = hfin[:, :, 0, S5_STATE:].reshape(1, bp, SSM_G, SSM_P)

    def lt_state(s):
        return s.reshape(bs, S5_NLT, S5_STATE).transpose(1, 0, 2)

    h0 = jnp.concatenate([lt_state(state_ssm_re[0]), lt_state(state_ssm_im[0])], axis=-1)
    zs, hs = _s5_step(us, h0, sw, ssm_d)
    s_ssm_re = hs[:, :, :S5_STATE].transpose(1, 0, 2).reshape(1, bs, SSM_G, SSM_P)
    s_ssm_im = hs[:, :, S5_STATE:].transpose(1, 0, 2).reshape(1, bs, SSM_G, SSM_P)

    def mem_kv_bf(x, b):
        return x.reshape(b, n_mem, MEM_W).astype(BF16)

    memp = _mem_attn(qmp.reshape(bp, tp, MEM_W), mem_kv_bf(p_mem_k[0], bp), mem_kv_bf(p_mem_v[0], bp), tm)
    mems = _mem_attn(qms.reshape(bs, 1, MEM_W), mem_kv_bf(cache_mem_k[0], bs), mem_kv_bf(cache_mem_v[0], bs), 1)

    x1p, rp = _mix(xp, zp.reshape(n_p, MAIN_W), memp.reshape(n_p, MEM_W), w_out_bf[0], ln1_g[0:1], ln1_b[0:1],
                   wr_bf[0], br[0], tm, w_glu_bf[0], b_glu[0:1])
    x1s, rs = _mix(xs, zs, mems.reshape(bs, MEM_W), w_out_bf[0], ln1_g[0:1], ln1_b[0:1],
                   wr_bf[0], br[0], bs, w_glu_bf[0], b_glu[0:1])
    x2p, x2s = _moe_layer([x1p, x1s], [rp, rs], wgu_bf[0], wd_bf[0], ln2_g[0:1], ln2_b[0:1])

    kp, vp, kpb, vpb, lfp, cp = _kvf(x2p, wkv_bf, wf_bf, bf_pad, bp, tm)
    ks, vs, _, _, lfs, _ = _kvf(x2s, wkv_bf, wf_bf, bf_pad, 1, bs)
    p_k = kp.reshape(bp, tp, FOX_H, FOX_DH)
    p_v = vp.reshape(bp, tp, FOX_H, FOX_DH)
    p_logf = lfp[:, :FOX_H].reshape(bp, tp, FOX_H)
    s_k = ks.reshape(bs, 1, FOX_H, FOX_DH)
    s_v = vs.reshape(bs, 1, FOX_H, FOX_DH)
    s_logf = lfs[:, :FOX_H].reshape(bs, 1, FOX_H)

    qp, qmp = _proj(x2p, w_in_bf[1], (MAIN_W, MEM_W), (BF16, BF16), tm)
    qs, qms = _proj(x2s, w_in_bf[1], (MAIN_W, MEM_W), (F32, BF16), bs)
    scale = FOX_DH ** -0.5
    c_hm = cp[:, :FOX_H].reshape(bp, tp, FOX_H).transpose(0, 2, 1)
    attn = _fox_prompt(_heads(qp * jnp.asarray(scale, BF16), bp, tp), _heads(kpb, bp, tp), _heads(vpb, bp, tp),
                       c_hm[:, :, None, :], 512)
    attn_p = attn.transpose(0, 2, 1, 3).reshape(n_p, MAIN_W)
    hpad = DEC_HEADS_PAD - FOX_H

    def pad_heads(x):
        return jnp.pad(x.reshape(bs, FOX_H, FOX_DH), ((0, 0), (0, hpad), (0, 0)))

    cache_lf_t = jnp.pad(cache_logf.transpose(0, 2, 1), ((0, 0), (0, hpad), (0, 0)))
    q_dec = pad_heads(qs * scale)
    attn_s = _fox_decode(page_table, q_dec, q_dec[..., None], pad_heads(ks), pad_heads(vs)[..., None],
                         lfs[:, :DEC_HEADS_PAD, None] * (jnp.arange(DEC_HEADS_PAD) < FOX_H)[None, :, None],
                         cache_k.transpose(0, 2, 3, 1), cache_v.transpose(0, 2, 3, 1),
                         cache_lf_t)[:, :FOX_H, :, 0].reshape(bs, MAIN_W)

    memp = _mem_attn(qmp.reshape(bp, tp, MEM_W), mem_kv_bf(p_mem_k[1], bp), mem_kv_bf(p_mem_v[1], bp), tm)
    mems = _mem_attn(qms.reshape(bs, 1, MEM_W), mem_kv_bf(cache_mem_k[1], bs), mem_kv_bf(cache_mem_v[1], bs), 1)
    x1p, rp = _mix(x2p, attn_p, memp.reshape(n_p, MEM_W), w_out_bf[1], ln1_g[1:2], ln1_b[1:2], wr_bf[1], br[1], tm)
    x1s, rs = _mix(x2s, attn_s, mems.reshape(bs, MEM_W), w_out_bf[1], ln1_g[1:2], ln1_b[1:2], wr_bf[1], br[1], bs)
    yp, ys = _moe_layer([x1p, x1s], [rp, rs], wgu_bf[1], wd_bf[1], ln2_g[1:2], ln2_b[1:2])

    return (yp.reshape(bp, tp, D_MODEL), ys.reshape(bs, 1, D_MODEL), p_ssm_re, p_ssm_im, p_k, p_v, p_logf,
            p_mem_k, p_mem_v, s_ssm_re, s_ssm_im, s_k, s_v, s_logf)
```

```python
import functools
import math

import jax
import jax.numpy as jnp
from jax import lax
from jax.experimental import pallas as pl
from jax.experimental.pallas import tpu as pltpu

F32 = jnp.float32
BF16 = jnp.bfloat16

D_MODEL = 1024
DEPTH = 2
MAIN_W = 768
MEM_W = 256
SSM_GC = 16
SSM_G = 48
SSM_P = 64
FOX_DH = 64
FOX_H = 12
MEM_H = 4
MEM_DH = 64
PAGE_SIZE = 128
MOE_GROUPS = 4
MOE_EPG = 4
MOE_F = 256
DN_ALPHA = (2.0 * DEPTH) ** 0.25
LN_EPS = 1e-5
NEG_INF = -1e30

LANES = 128
S5_CHUNK = 8
S5_GPL = LANES // SSM_GC
S5_NLT = MAIN_W // LANES
S5_STATE = S5_GPL * SSM_P
VMEM_LIMIT = 56 * 1024 * 1024


def _cparams(sem):
    return pltpu.CompilerParams(dimension_semantics=sem, vmem_limit_bytes=VMEM_LIMIT)


def _proj_body(x_ref, w_ref, *o_refs, widths):
    y = jnp.dot(x_ref[...].astype(BF16), w_ref[...], preferred_element_type=F32)
    off = 0
    for o_ref, n in zip(o_refs, widths):
        o_ref[...] = y[:, off:off + n].astype(o_ref.dtype)
        off += n


def _proj(x, w_bf, widths, dtypes, tm):
    m, k = x.shape
    n = w_bf.shape[1]
    assert m % tm == 0 and sum(widths) == n
    return pl.pallas_call(
        functools.partial(_proj_body, widths=tuple(widths)),
        grid=(m // tm,),
        in_specs=[pl.BlockSpec((tm, k), lambda i: (i, 0)),
                  pl.BlockSpec((k, n), lambda i: (0, 0))],
        out_specs=[pl.BlockSpec((tm, w), lambda i: (i, 0)) for w in widths],
        out_shape=[jax.ShapeDtypeStruct((m, w), dt) for w, dt in zip(widths, dtypes)],
        compiler_params=_cparams(("arbitrary",)),
        name="proj",
    )(x, w_bf)


def _gelu(x):
    c = math.sqrt(2.0 / math.pi)
    return 0.5 * x * (1.0 + jnp.tanh(c * (x + 0.044715 * (x * x * x))))


def _s5_weights(a_re, a_im, log_dt, b_re, b_im, c_re, c_im):
    f32 = F32
    L = S5_CHUNK
    a_re, a_im = a_re.astype(f32), a_im.astype(f32)
    dt = jnp.exp(log_dt.astype(f32))[:, None]
    mag = jnp.exp(a_re * dt)
    lb_re = mag * jnp.cos(a_im * dt)
    lb_im = mag * jnp.sin(a_im * dt)
    den = a_re * a_re + a_im * a_im
    zr = ((lb_re - 1.0) * a_re + lb_im * a_im) / den
    zi = (lb_im * a_re - (lb_re - 1.0) * a_im) / den
    b_re, b_im = b_re.astype(f32), b_im.astype(f32)
    bb_re = zr[..., None] * b_re - zi[..., None] * b_im
    bb_im = zr[..., None] * b_im + zi[..., None] * b_re
    ks = jnp.arange(L + 1, dtype=f32)[:, None, None]
    mag_k = jnp.exp(a_re * dt * ks)
    pk_re = mag_k * jnp.cos(a_im * dt * ks)
    pk_im = mag_k * jnp.sin(a_im * dt * ks)
    c_re, c_im = c_re.astype(f32), c_im.astype(f32)

    eye = jnp.eye(S5_GPL, dtype=f32)

    pb_re = pk_re[..., None] * bb_re - pk_im[..., None] * bb_im
    pb_im = pk_re[..., None] * bb_im + pk_im[..., None] * bb_re

    def lt_w1(pb):
        w = pb[L - 1 - jnp.arange(L)]
        w = w.transpose(1, 0, 3, 2).reshape(S5_NLT, S5_GPL, L, SSM_GC, SSM_P)
        w = w.transpose(0, 2, 1, 3, 4)[:, :, :, :, None, :] * eye[None, None, :, None, :, None]
        return w.reshape(S5_NLT, L * LANES, S5_STATE)

    w1 = jnp.concatenate([lt_w1(pb_re), lt_w1(pb_im)], axis=-1)

    kk = (jnp.sum(pb_re[:L].transpose(0, 1, 3, 2)[:, :, :, None, :] * c_re[None, :, None, :, :], axis=-1)
          - jnp.sum(pb_im[:L].transpose(0, 1, 3, 2)[:, :, :, None, :] * c_im[None, :, None, :, :], axis=-1))
    ti = jnp.arange(L)[:, None]
    to = jnp.arange(L)[None, :]
    lag = jnp.clip(to - ti, 0, L - 1)
    mm = jnp.where((to >= ti)[:, :, None, None, None], kk[lag], 0.0)
    mm = mm.transpose(2, 0, 3, 1, 4).reshape(S5_NLT, S5_GPL, L, SSM_GC, L, SSM_GC)
    mm = mm.transpose(0, 2, 1, 3, 4, 5)[:, :, :, :, :, None, :] * eye[None, None, :, None, None, :, None]
    mm = mm.reshape(S5_NLT, L * LANES, L * LANES)

    def lt_w2(x):
        w = x.transpose(1, 3, 0, 2).reshape(S5_NLT, S5_GPL, SSM_P, L, SSM_GC)
        w = w[:, :, :, :, None, :] * eye[None, :, None, None, :, None]
        return w.reshape(S5_NLT, S5_STATE, L * LANES)

    ca_re = c_re[None] * pk_re[1:, :, None, :] - c_im[None] * pk_im[1:, :, None, :]
    ca_im = c_re[None] * pk_im[1:, :, None, :] + c_im[None] * pk_re[1:, :, None, :]
    w2 = jnp.concatenate([lt_w2(ca_re), lt_w2(-ca_im)], axis=1)

    def lt_vec(x):
        return x.reshape(S5_NLT, 1, S5_STATE)

    a_chunk = jnp.concatenate([lt_vec(pk_re[L]), lt_vec(pk_im[L])], axis=-1)
    a_one = jnp.concatenate([lt_vec(pk_re[1]), lt_vec(pk_im[1])], axis=-1)

    def lt_c0(x):
        w = x.transpose(0, 2, 1).reshape(S5_NLT, S5_GPL, SSM_P, SSM_GC)
        w = w[:, :, :, None, :] * eye[None, :, None, :, None]
        return w.reshape(S5_NLT, S5_STATE, LANES)

    c0 = jnp.concatenate([lt_c0(c_re), lt_c0(-c_im)], axis=1)
    bb1 = w1[:, (L - 1) * LANES:, :]
    return dict(w1m=jnp.concatenate([w1, mm], axis=-1).astype(BF16), w2=w2.astype(BF16),
                a_chunk=a_chunk, a_one=a_one, c0=c0.astype(BF16), bb1=bb1.astype(BF16))


def _s5_body(u_ref, w1m_ref, w2_ref, ach_ref, d_ref, z_ref, hfin_ref, s_scr, hp_scr, h_scr, *, tm):
    t = pl.program_id(2)
    nc = tm // S5_CHUNK
    ns = S5_STATE

    @pl.when(t == 0)
    def _():
        h_scr[...] = jnp.zeros_like(h_scr)

    parts = [u_ref[0, pl.ds(k, nc, stride=S5_CHUNK), :] for k in range(S5_CHUNK)]
    a = jnp.concatenate(parts, axis=1).astype(BF16)
    sy = jnp.dot(a, w1m_ref[0], preferred_element_type=F32)
    s_scr[...] = sy[:, :2 * ns]
    ar = ach_ref[0, :, :ns]
    ai = ach_ref[0, :, ns:]

    def step(j, carry):
        hr, hi = carry
        hp_scr[pl.ds(j, 1), :ns] = hr
        hp_scr[pl.ds(j, 1), ns:] = hi
        sr = s_scr[pl.ds(j, 1), :ns]
        si = s_scr[pl.ds(j, 1), ns:]
        return ar * hr - ai * hi + sr, ar * hi + ai * hr + si

    hr, hi = lax.fori_loop(0, nc, step, (h_scr[:, :ns], h_scr[:, ns:]), unroll=8)
    h_scr[:, :ns] = hr
    h_scr[:, ns:] = hi
    hfin_ref[0, 0] = h_scr[...]

    yi = jnp.dot(hp_scr[...].astype(BF16), w2_ref[0], preferred_element_type=F32)
    d = d_ref[...]
    for k in range(S5_CHUNK):
        yk = sy[:, 2 * ns + k * LANES:2 * ns + (k + 1) * LANES] + yi[:, k * LANES:(k + 1) * LANES] + d * parts[k]
        z_ref[0, pl.ds(k, nc, stride=S5_CHUNK), :] = _gelu(yk)


def _s5_prompt(u, sw, d_skip, tm):
    b, t, _ = u.shape
    assert t % tm == 0 and tm % (8 * S5_CHUNK) == 0
    nc = tm // S5_CHUNK
    ns2 = 2 * S5_STATE
    kw = S5_CHUNK * LANES
    return pl.pallas_call(
        functools.partial(_s5_body, tm=tm),
        grid=(S5_NLT, b, t // tm),
        in_specs=[pl.BlockSpec((1, tm, LANES), lambda l, bb, tt: (bb, tt, l)),
                  pl.BlockSpec((1, kw, ns2 + kw), lambda l, bb, tt: (l, 0, 0)),
                  pl.BlockSpec((1, ns2, kw), lambda l, bb, tt: (l, 0, 0)),
                  pl.BlockSpec((1, 1, ns2), lambda l, bb, tt: (l, 0, 0)),
                  pl.BlockSpec((1, LANES), lambda l, bb, tt: (0, l))],
        out_specs=[pl.BlockSpec((1, tm, LANES), lambda l, bb, tt: (bb, tt, l)),
                   pl.BlockSpec((1, 1, 1, ns2), lambda l, bb, tt: (bb, l, 0, 0))],
        out_shape=[jax.ShapeDtypeStruct(u.shape, F32),
                   jax.ShapeDtypeStruct((b, S5_NLT, 1, ns2), F32)],
        scratch_shapes=[pltpu.VMEM((nc, ns2), F32), pltpu.VMEM((nc, ns2), F32), pltpu.VMEM((1, ns2), F32)],
        compiler_params=_cparams(("arbitrary", "arbitrary", "arbitrary")),
        name="s5_prompt",
    )(u, sw["w1m"], sw["w2"], sw["a_chunk"], d_skip)


def _s5_step_body(u_ref, h0_ref, bb1_ref, c0_ref, a1_ref, d_ref, z_ref, h_ref):
    ns = S5_STATE
    for l in range(S5_NLT):
        u = u_ref[:, l * LANES:(l + 1) * LANES]
        bu = jnp.dot(u.astype(BF16), bb1_ref[l], preferred_element_type=F32)
        ar = a1_ref[l, :, :ns]
        ai = a1_ref[l, :, ns:]
        h0r = h0_ref[l, :, :ns]
        h0i = h0_ref[l, :, ns:]
        hr = ar * h0r - ai * h0i + bu[:, :ns]
        hi = ar * h0i + ai * h0r + bu[:, ns:]
        h_ref[l, :, :ns] = hr
        h_ref[l, :, ns:] = hi
        h = jnp.concatenate([hr, hi], axis=1).astype(BF16)
        y = jnp.dot(h, c0_ref[l], preferred_element_type=F32) + d_ref[:, l * LANES:(l + 1) * LANES] * u
        z_ref[:, l * LANES:(l + 1) * LANES] = _gelu(y)


def _s5_step(u, h0, sw, d_skip):
    n = u.shape[0]
    return pl.pallas_call(
        _s5_step_body,
        out_shape=[jax.ShapeDtypeStruct((n, MAIN_W), F32),
                   jax.ShapeDtypeStruct((S5_NLT, n, 2 * S5_STATE), F32)],
        compiler_params=pltpu.CompilerParams(vmem_limit_bytes=VMEM_LIMIT),
        name="s5_step",
    )(u, h0, sw["bb1"], sw["c0"], sw["a_one"], d_skip)


def _mem_attn_body(q_ref, k_ref, v_ref, o_ref):
    q = q_ref[0]
    tm = q.shape[0]
    rows = max(tm, 8)
    outs = []
    for h in range(MEM_H):
        sl = slice(h * MEM_DH, (h + 1) * MEM_DH)
        qh = q[:, sl]
        if tm < rows:
            qh = jnp.broadcast_to(qh, (rows, MEM_DH))
        s = lax.dot_general(qh, k_ref[0, :, sl], (((1,), (1,)), ((), ())),
                            preferred_element_type=F32) * (MEM_DH ** -0.5)
        m = jnp.max(s, axis=1, keepdims=True)
        e = jnp.exp(s - m)
        p = e / jnp.sum(e, axis=1, keepdims=True)
        o = jnp.dot(p.astype(BF16), v_ref[0, :, sl], preferred_element_type=F32)
        outs.append(o[:tm])
    o_ref[0] = jnp.concatenate(outs, axis=1).astype(o_ref.dtype)


def _mem_attn(q, k, v, tm):
    b, t, w = q.shape
    nm = k.shape[1]
    return pl.pallas_call(
        _mem_attn_body,
        grid=(b, t // tm),
        in_specs=[pl.BlockSpec((1, tm, w), lambda i, j: (i, j, 0)),
                  pl.BlockSpec((1, nm, w), lambda i, j: (i, 0, 0)),
                  pl.BlockSpec((1, nm, w), lambda i, j: (i, 0, 0))],
        out_specs=pl.BlockSpec((1, tm, w), lambda i, j: (i, j, 0)),
        out_shape=jax.ShapeDtypeStruct((b, t, w), BF16),
        compiler_params=_cparams(("arbitrary", "arbitrary")),
        name="mem_attn",
    )(q, k, v)


def _layer_norm(x, g, b):
    mu = jnp.mean(x, axis=-1, keepdims=True)
    xc = x - mu
    var = jnp.mean(xc * xc, axis=-1, keepdims=True)
    return xc * lax.rsqrt(var + LN_EPS) * g + b


def _route(logits):
    lane = lax.broadcasted_iota(jnp.int32, logits.shape, 1).astype(F32)
    lo = jnp.float32(jnp.finfo(jnp.float32).min)
    none = jnp.float32(LANES)
    is_g = lane < MOE_GROUPS
    gl = jnp.where(is_g, logits, lo)
    gm = jnp.max(gl, axis=1, keepdims=True)
    gsum = jnp.sum(jnp.where(is_g, jnp.exp(gl - gm), 0.0), axis=1, keepdims=True)
    p_g = 1.0 / gsum
    g_idx = jnp.min(jnp.where(is_g & (gl == gm), lane, none), axis=1, keepdims=True)
    e_lo = MOE_GROUPS + MOE_EPG * g_idx
    is_e = (lane >= e_lo) & (lane < e_lo + MOE_EPG)
    el = jnp.where(is_e, logits, lo)
    v1 = jnp.max(el, axis=1, keepdims=True)
    i1 = jnp.min(jnp.where(is_e & (el == v1), lane, none), axis=1, keepdims=True)
    is_e2 = is_e & (lane != i1)
    el2 = jnp.where(is_e2, logits, lo)
    v2 = jnp.max(el2, axis=1, keepdims=True)
    i2 = jnp.min(jnp.where(is_e2 & (el2 == v2), lane, none), axis=1, keepdims=True)
    r = jnp.exp(v2 - v1)
    w1 = p_g / (1.0 + r)
    w2 = p_g * r / (1.0 + r)
    out = jnp.where(lane == i1, w1, 0.0) + jnp.where(lane == i2, w2, 0.0)
    return out + jnp.where(lane == 0.0, g_idx, 0.0)


def _mix_body(*refs, glu):
    if glu:
        (x_ref, main_ref, mem_ref, wglu_ref, bglu_ref, wout_ref, g_ref, b_ref, wr_ref, br_ref,
         x1_ref, route_ref) = refs
        z = main_ref[...]
        gate = jnp.dot(z.astype(BF16), wglu_ref[...], preferred_element_type=F32) + bglu_ref[...]
        main = (z * jax.nn.sigmoid(gate)).astype(BF16)
    else:
        x_ref, main_ref, mem_ref, wout_ref, g_ref, b_ref, wr_ref, br_ref, x1_ref, route_ref = refs
        main = main_ref[...].astype(BF16)
    mixed = (jnp.dot(main, wout_ref[:MAIN_W, :], preferred_element_type=F32)
             + jnp.dot(mem_ref[...], wout_ref[MAIN_W:, :], preferred_element_type=F32))
    x1 = _layer_norm(DN_ALPHA * x_ref[...] + mixed, g_ref[...], b_ref[...])
    x1_ref[...] = x1
    logits = jnp.dot(x1.astype(BF16), wr_ref[...], preferred_element_type=F32) + br_ref[...]
    route_ref[...] = _route(logits)


def _mix(x, main, mem, w_out_bf, ln_g, ln_b, wr_bf, br, tm, w_glu_bf=None, b_glu=None):
    n = x.shape[0]
    glu = w_glu_bf is not None
    row = lambda w: pl.BlockSpec((tm, w), lambda i: (i, 0))
    full = lambda a: pl.BlockSpec(a.shape, lambda i: (0,) * a.ndim)
    args = [x, main, mem]
    specs = [row(D_MODEL), row(MAIN_W), row(MEM_W)]
    if glu:
        args += [w_glu_bf, b_glu]
        specs += [full(w_glu_bf), full(b_glu)]
    tail = [w_out_bf, ln_g, ln_b, wr_bf, br]
    args += tail
    specs += [full(a) for a in tail]
    return pl.pallas_call(
        functools.partial(_mix_body, glu=glu),
        grid=(n // tm,),
        in_specs=specs,
        out_specs=[row(D_MODEL), row(LANES)],
        out_shape=[jax.ShapeDtypeStruct((n, D_MODEL), F32), jax.ShapeDtypeStruct((n, LANES), F32)],
        compiler_params=_cparams(("arbitrary",)),
        name="mix_glu" if glu else "mix",
    )(*args)


def _moe_body(tg_ref, x_ref, cw_ref, wgu_ref, wd_ref, g_ref, b_ref, o_ref):
    del tg_ref
    x = x_ref[...]
    hf = MOE_EPG * MOE_F
    gu = jnp.dot(x.astype(BF16), wgu_ref[0], preferred_element_type=F32)
    h = jax.nn.silu(gu[:, :hf]) * gu[:, hf:]
    cw = cw_ref[...]
    hs = [h[:, e * MOE_F:(e + 1) * MOE_F] * cw[:, e:e + 1] for e in range(MOE_EPG)]
    hb = jnp.concatenate(hs, axis=1).astype(BF16)
    y = jnp.dot(hb, wd_ref[0], preferred_element_type=F32)
    o_ref[...] = _layer_norm(DN_ALPHA * x + y, g_ref[...], b_ref[...])


def _moe(tile_group, xs, cws, wgu_bf, wd_bf, ln_g, ln_b, tm):
    p = xs.shape[0]
    hf = MOE_EPG * MOE_F
    grid_spec = pltpu.PrefetchScalarGridSpec(
        num_scalar_prefetch=1,
        grid=(p // tm,),
        in_specs=[pl.BlockSpec((tm, D_MODEL), lambda i, tg: (i, 0)),
                  pl.BlockSpec((tm, MOE_EPG), lambda i, tg: (i, 0)),
                  pl.BlockSpec((1, D_MODEL, 2 * hf), lambda i, tg: (tg[i], 0, 0)),
                  pl.BlockSpec((1, hf, D_MODEL), lambda i, tg: (tg[i], 0, 0)),
                  pl.BlockSpec((1, D_MODEL), lambda i, tg: (0, 0)),
                  pl.BlockSpec((1, D_MODEL), lambda i, tg: (0, 0))],
        out_specs=pl.BlockSpec((tm, D_MODEL), lambda i, tg: (i, 0)),
    )
    return pl.pallas_call(
        _moe_body,
        grid_spec=grid_spec,
        out_shape=jax.ShapeDtypeStruct((p, D_MODEL), F32),
        compiler_params=_cparams(("arbitrary",)),
        name="moe",
    )(tile_group, xs, cws, wgu_bf, wd_bf, ln_g, ln_b)


MOE_TM = 256


def _moe_layer(x1_parts, route_parts, wgu_bf, wd_bf, ln_g, ln_b):
    x1 = jnp.concatenate(x1_parts, axis=0)
    route = jnp.concatenate(route_parts, axis=0)
    n = x1.shape[0]
    tm = MOE_TM
    g_idx = route[:, 0].astype(jnp.int32)
    cw16 = route[:, MOE_GROUPS:MOE_GROUPS + MOE_GROUPS * MOE_EPG].reshape(n, MOE_GROUPS, MOE_EPG)
    cw = jnp.take_along_axis(cw16, g_idx[:, None, None], axis=1)[:, 0, :]
    onehot = (g_idx[:, None] == jnp.arange(MOE_GROUPS)[None, :]).astype(jnp.int32)
    counts = jnp.sum(onehot, axis=0)
    rank = jnp.sum((jnp.cumsum(onehot, axis=0) - onehot) * onehot, axis=1)
    tiles = (counts + tm - 1) // tm
    tile_start = jnp.cumsum(tiles) - tiles
    pos = tile_start[g_idx] * tm + rank
    n_tiles = (n + tm - 1) // tm + MOE_GROUPS
    p = n_tiles * tm
    src = jnp.full((p,), n, jnp.int32).at[pos].set(jnp.arange(n, dtype=jnp.int32))
    tile_end = jnp.cumsum(tiles)
    tile_group = jnp.minimum(jnp.sum((jnp.arange(n_tiles)[:, None] >= tile_end[None, :]).astype(jnp.int32), axis=1),
                             MOE_GROUPS - 1).astype(jnp.int32)
    xs = x1[jnp.minimum(src, n - 1)]
    cws = jnp.concatenate([cw, jnp.zeros((1, MOE_EPG), F32)], axis=0)[src]
    ys = _moe(tile_group, xs, cws, wgu_bf, wd_bf, ln_g, ln_b, tm)
    y = ys[pos]
    outs, off = [], 0
    for part in x1_parts:
        outs.append(y[off:off + part.shape[0]])
        off += part.shape[0]
    return outs


def _log_sigmoid(x):
    return -(jnp.maximum(-x, 0.0) + jnp.log1p(jnp.exp(-jnp.abs(x))))


def _split3(x):
    a1 = x.astype(BF16)
    r1 = x - a1.astype(F32)
    a2 = r1.astype(BF16)
    a3 = (r1 - a2.astype(F32)).astype(BF16)
    return a1, a2, a3


def _kvf_body(x_ref, wkv_ref, wf_ref, bf_ref, k_ref, v_ref, kb_ref, vb_ref, lf_ref, c_ref, carry):
    t = pl.program_id(1)

    @pl.when(t == 0)
    def _():
        carry[...] = jnp.zeros_like(carry)

    xb = x_ref[...].astype(BF16)
    tm = xb.shape[0]
    kv = jnp.dot(xb, wkv_ref[...], preferred_element_type=F32)
    k = kv[:, :MAIN_W]
    v = kv[:, MAIN_W:]
    k_ref[...] = k
    v_ref[...] = v
    kb_ref[...] = k.astype(BF16)
    vb_ref[...] = v.astype(BF16)
    lf = _log_sigmoid(jnp.dot(xb, wf_ref[...], preferred_element_type=F32) + bf_ref[...])
    lf_ref[...] = lf
    r = lax.broadcasted_iota(jnp.int32, (tm, tm), 0)
    c = lax.broadcasted_iota(jnp.int32, (tm, tm), 1)
    tri = (c <= r).astype(BF16)
    a1, a2, a3 = _split3(lf)
    cs = (jnp.dot(tri, a1, preferred_element_type=F32) + jnp.dot(tri, a2, preferred_element_type=F32)
          + jnp.dot(tri, a3, preferred_element_type=F32)) + carry[...]
    c_ref[...] = cs
    carry[...] = cs[tm - 1:tm, :]


def _kvf(x, wkv_bf, wf_bf, bf_pad, nb, tm):
    n = x.shape[0]
    t = n // nb
    assert t % tm == 0
    steps = t // tm
    row = lambda w: pl.BlockSpec((tm, w), lambda b, i: (b * steps + i, 0))
    full = lambda a: pl.BlockSpec(a.shape, lambda b, i: (0,) * a.ndim)
    return pl.pallas_call(
        _kvf_body,
        grid=(nb, steps),
        in_specs=[row(D_MODEL), full(wkv_bf), full(wf_bf), full(bf_pad)],
        out_specs=[row(MAIN_W), row(MAIN_W), row(MAIN_W), row(MAIN_W), row(LANES), row(LANES)],
        out_shape=[jax.ShapeDtypeStruct((n, MAIN_W), F32), jax.ShapeDtypeStruct((n, MAIN_W), F32),
                   jax.ShapeDtypeStruct((n, MAIN_W), BF16), jax.ShapeDtypeStruct((n, MAIN_W), BF16),
                   jax.ShapeDtypeStruct((n, LANES), F32), jax.ShapeDtypeStruct((n, LANES), F32)],
        scratch_shapes=[pltpu.VMEM((1, LANES), F32)],
        compiler_params=_cparams(("arbitrary", "arbitrary")),
        name="kvf",
    )(x, wkv_bf, wf_bf, bf_pad)


FOX_ROW_SPLITS = 2


def _fox_body(q_ref, k_ref, v_ref, ck_ref, o_ref, m_scr, l_scr, acc_scr, *, tq):
    i = pl.program_id(2)
    rq = tq // FOX_ROW_SPLITS
    m_scr[...] = jnp.full_like(m_scr, NEG_INF)
    l_scr[...] = jnp.zeros_like(l_scr)
    acc_scr[...] = jnp.zeros_like(acc_scr)

    def block(j, masked):
        start = pl.multiple_of(j * tq, tq)
        for part in range(FOX_ROW_SPLITS):
            rows = slice(part * rq, (part + 1) * rq)
            nk = (part + 1) * rq if masked else tq
            q = q_ref[0, 0, rows, :]
            k = k_ref[0, 0, pl.ds(start, nk), :]
            v = v_ref[0, 0, pl.ds(start, nk), :]
            s = lax.dot_general(q, k, (((1,), (1,)), ((), ())), preferred_element_type=F32)
            s = s - ck_ref[0, 0, :, pl.ds(start, nk)]
            if masked:
                r = lax.broadcasted_iota(jnp.int32, (rq, nk), 0) + part * rq
                c = lax.broadcasted_iota(jnp.int32, (rq, nk), 1)
                s = jnp.where(c <= r, s, NEG_INF)
            m_prev = m_scr[rows, :]
            m_new = jnp.maximum(m_prev, jnp.max(s, axis=1, keepdims=True))
            alpha = jnp.exp(m_prev - m_new)
            p = jnp.exp(s - m_new)
            l_scr[rows, :] = alpha * l_scr[rows, :] + jnp.sum(p, axis=1, keepdims=True)
            acc_scr[rows, :] = alpha * acc_scr[rows, :] + jnp.dot(p.astype(BF16), v, preferred_element_type=F32)
            m_scr[rows, :] = m_new

    def body(j, carry):
        block(j, False)
        return carry

    lax.fori_loop(0, i, body, 0)
    block(i, True)
    o_ref[0, 0] = (acc_scr[...] / l_scr[...]).astype(o_ref.dtype)


def _fox_prompt(q, k, v, c_row, tq):
    b, h, t, dh = q.shape
    assert t % tq == 0 and tq % (FOX_ROW_SPLITS * LANES) == 0
    return pl.pallas_call(
        functools.partial(_fox_body, tq=tq),
        grid=(b, h, t // tq),
        in_specs=[pl.BlockSpec((1, 1, tq, dh), lambda bi, hi, i: (bi, hi, i, 0)),
                  pl.BlockSpec((1, 1, t, dh), lambda bi, hi, i: (bi, hi, 0, 0)),
                  pl.BlockSpec((1, 1, t, dh), lambda bi, hi, i: (bi, hi, 0, 0)),
                  pl.BlockSpec((1, 1, 1, t), lambda bi, hi, i: (bi, hi, 0, 0))],
        out_specs=pl.BlockSpec((1, 1, tq, dh), lambda bi, hi, i: (bi, hi, i, 0)),
        out_shape=jax.ShapeDtypeStruct((b, h, t, dh), BF16),
        scratch_shapes=[pltpu.VMEM((tq, 1), F32), pltpu.VMEM((tq, 1), F32), pltpu.VMEM((tq, dh), F32)],
        compiler_params=_cparams(("arbitrary", "arbitrary", "arbitrary")),
        name="fox_prompt",
    )(q, k, v, c_row)


DEC_PAGES_PER_STEP = 8
DEC_HEADS_PAD = 16


def _fox_decode_body(pt_ref, q_ref, qc_ref, kn_ref, vnc_ref, lfn_ref, *refs, npp):
    del pt_ref
    k_refs = refs[:npp]
    v_refs = refs[npp:2 * npp]
    lf_refs = refs[2 * npp:3 * npp]
    o_ref = refs[3 * npp]
    m_scr, l_scr, acc_scr, c_scr, s_scr, cs_scr = refs[3 * npp + 1:]
    j = pl.program_id(1)
    nj = pl.num_programs(1)

    @pl.when(j == 0)
    def _():
        m_scr[...] = jnp.full_like(m_scr, NEG_INF)
        l_scr[...] = jnp.zeros_like(l_scr)
        acc_scr[...] = jnp.zeros_like(acc_scr)
        c_scr[...] = jnp.zeros_like(c_scr)
        s_scr[...] = jnp.zeros_like(s_scr)

    r = lax.broadcasted_iota(jnp.int32, (PAGE_SIZE, PAGE_SIZE), 0)
    c = lax.broadcasted_iota(jnp.int32, (PAGE_SIZE, PAGE_SIZE), 1)
    tri = (r <= c).astype(BF16)

    carry = c_scr[...]
    for pg in range(npp):
        a1, a2, a3 = _split3(lf_refs[pg][0])
        cs = (jnp.dot(a1, tri, preferred_element_type=F32) + jnp.dot(a2, tri, preferred_element_type=F32)
              + jnp.dot(a3, tri, preferred_element_type=F32)) + carry
        cs_scr[:, pg * PAGE_SIZE:(pg + 1) * PAGE_SIZE] = cs
        carry = cs[:, PAGE_SIZE - 1:PAGE_SIZE]
    c_scr[...] = carry

    for h in range(FOX_H):
        qb = jnp.broadcast_to(qc_ref[0, h], (FOX_DH, PAGE_SIZE))
        for pg in range(npp):
            s_scr[h:h + 1, pg * PAGE_SIZE:(pg + 1) * PAGE_SIZE] = jnp.sum(
                k_refs[pg][0, h] * qb, axis=0, keepdims=True)

    s = s_scr[...] - cs_scr[...]
    m_prev = m_scr[...]
    m_new = jnp.maximum(m_prev, jnp.max(s, axis=1, keepdims=True))
    alpha = jnp.exp(m_prev - m_new)
    p = jnp.exp(s - m_new)
    l_scr[...] = alpha * l_scr[...] + jnp.sum(p, axis=1, keepdims=True)
    m_scr[...] = m_new
    cs_scr[...] = p
    for h in range(FOX_H):
        acc = acc_scr[h] * alpha[h:h + 1, :]
        for pg in range(npp):
            acc = acc + v_refs[pg][0, h] * cs_scr[h:h + 1, pg * PAGE_SIZE:(pg + 1) * PAGE_SIZE]
        acc_scr[h] = acc

    @pl.when(j == nj - 1)
    def _():
        s_new = jnp.sum(q_ref[0] * kn_ref[0], axis=1, keepdims=True) - (c_scr[...] + lfn_ref[0])
        m_last = m_scr[...]
        m_fin = jnp.maximum(m_last, s_new)
        a_fin = jnp.exp(m_last - m_fin)
        p_new = jnp.exp(s_new - m_fin)
        l_fin = a_fin * l_scr[...] + p_new
        o_ref[...] = jnp.zeros_like(o_ref)
        for h in range(FOX_H):
            tot = jnp.sum(acc_scr[h], axis=1, keepdims=True)
            o_ref[0, h] = (a_fin[h:h + 1, :] * tot + p_new[h:h + 1, :] * vnc_ref[0, h]) / l_fin[h:h + 1, :]


def _fox_decode(page_table, q, q_col, k_new, v_new_col, lf_new, cache_kt, cache_vt, cache_lf_t):
    b, n_pages = page_table.shape
    npp = DEC_PAGES_PER_STEP
    hp = DEC_HEADS_PAD
    assert n_pages % npp == 0
    tok = lambda w: pl.BlockSpec((1, hp, w), lambda bi, j, pt: (bi, 0, 0))
    col = pl.BlockSpec((1, hp, FOX_DH, 1), lambda bi, j, pt: (bi, 0, 0, 0))

    def page_spec(shape_tail, pg):
        nd = len(shape_tail)
        return pl.BlockSpec((1,) + shape_tail, lambda bi, j, pt, pg=pg: (pt[bi, j * npp + pg],) + (0,) * nd)

    in_specs = [tok(FOX_DH), col, tok(FOX_DH), col, tok(1)]
    in_specs += [page_spec((FOX_H, FOX_DH, PAGE_SIZE), pg) for pg in range(npp)]
    in_specs += [page_spec((FOX_H, FOX_DH, PAGE_SIZE), pg) for pg in range(npp)]
    in_specs += [page_spec((hp, PAGE_SIZE), pg) for pg in range(npp)]
    grid_spec = pltpu.PrefetchScalarGridSpec(
        num_scalar_prefetch=1,
        grid=(b, n_pages // npp),
        in_specs=in_specs,
        out_specs=col,
        scratch_shapes=[pltpu.VMEM((hp, 1), F32), pltpu.VMEM((hp, 1), F32),
                        pltpu.VMEM((FOX_H, FOX_DH, PAGE_SIZE), F32), pltpu.VMEM((hp, 1), F32),
                        pltpu.VMEM((hp, npp * PAGE_SIZE), F32), pltpu.VMEM((hp, npp * PAGE_SIZE), F32)],
    )
    return pl.pallas_call(
        functools.partial(_fox_decode_body, npp=npp),
        grid_spec=grid_spec,
        out_shape=jax.ShapeDtypeStruct((b, hp, FOX_DH, 1), F32),
        compiler_params=_cparams(("arbitrary", "arbitrary")),
        name="fox_decode",
    )(page_table, q, q_col, k_new, v_new_col, lf_new,
      *([cache_kt] * npp), *([cache_vt] * npp), *([cache_lf_t] * npp))


def _heads(x, b, t):
    return x.reshape(b, t, FOX_H, FOX_DH).transpose(0, 2, 1, 3)


def kernel(x_prompt, x_sample, cache_k, cache_v, cache_logf, cache_mem_k, cache_mem_v, state_ssm_re, state_ssm_im, page_table, mem_prompt, w_in, w_out, ln1_g, ln1_b, ln2_g, ln2_b, w_mem_kv, ssm_a_re, ssm_a_im, ssm_log_dt, ssm_b_re, ssm_b_im, ssm_c_re, ssm_c_im, ssm_d, w_glu, b_glu, w_kvf, b_f, w_r1, b_r1, w_r2, b_r2, w_e_gate, w_e_up, w_e_down):
    bp, tp, _ = x_prompt.shape
    bs = x_sample.shape[0]
    n_p = bp * tp
    n_mem = mem_prompt.shape[1]
    tm = 512

    w_in_bf = w_in.astype(BF16)
    w_out_bf = w_out.astype(BF16)
    w_glu_bf = w_glu.astype(BF16)
    wkv_bf = w_kvf[:, :2 * MAIN_W].astype(BF16)
    wf_bf = jnp.pad(w_kvf[:, 2 * MAIN_W:], ((0, 0), (0, LANES - FOX_H))).astype(BF16)
    bf_pad = jnp.pad(b_f, (0, LANES - FOX_H))[None, :]
    n_e = MOE_GROUPS * MOE_EPG
    wr = jnp.concatenate([w_r1, w_r2.transpose(0, 2, 1, 3).reshape(DEPTH, D_MODEL, n_e)], axis=-1)
    wr_bf = jnp.pad(wr, ((0, 0), (0, 0), (0, LANES - MOE_GROUPS - n_e))).astype(BF16)
    br = jnp.pad(jnp.concatenate([b_r1, b_r2.reshape(DEPTH, n_e)], axis=-1),
                 ((0, 0), (0, LANES - MOE_GROUPS - n_e)))[:, None, :]
    hf = MOE_EPG * MOE_F

    def group_cols(w):
        return w.reshape(DEPTH, MOE_GROUPS, MOE_EPG, D_MODEL, MOE_F).transpose(0, 1, 3, 2, 4).reshape(
            DEPTH, MOE_GROUPS, D_MODEL, hf)

    wgu_bf = jnp.concatenate([group_cols(w_e_gate), group_cols(w_e_up)], axis=-1).astype(BF16)
    wd_bf = w_e_down.reshape(DEPTH, MOE_GROUPS, hf, D_MODEL).astype(BF16)
    sw = _s5_weights(ssm_a_re[0], ssm_a_im[0], ssm_log_dt[0], ssm_b_re[0], ssm_b_im[0], ssm_c_re[0], ssm_c_im[0])

    mem2d = mem_prompt.reshape(bp * n_mem, D_MODEL)
    p_mem = [_proj(mem2d, w_mem_kv[l].astype(BF16), (MEM_W, MEM_W), (F32, F32), tm) for l in range(DEPTH)]
    p_mem_k = jnp.stack([pm[0].reshape(bp, n_mem, MEM_H, MEM_DH) for pm in p_mem])
    p_mem_v = jnp.stack([pm[1].reshape(bp, n_mem, MEM_H, MEM_DH) for pm in p_mem])

    xp = x_prompt.reshape(n_p, D_MODEL)
    xs = x_sample.reshape(bs, D_MODEL)

    up, qmp = _proj(xp, w_in_bf[0], (MAIN_W, MEM_W), (F32, BF16), tm)
    us, qms = _proj(xs, w_in_bf[0], (MAIN_W, MEM_W), (F32, BF16), bs)
    zp, hfin = _s5_prompt(up.reshape(bp, tp, MAIN_W), sw, ssm_d, 2048)
    p_ssm_re = hfin[:, :, 0, :S5_STATE].reshape(1, bp, SSM_G, SSM_P)
    p_ssm_im = hfin[:, :, 0, S5_STATE:].reshape(1, bp, SSM_G, SSM_P)

    def lt_state(s):
        return s.reshape(bs, S5_NLT, S5_STATE).transpose(1, 0, 2)

    h0 = jnp.concatenate([lt_state(state_ssm_re[0]), lt_state(state_ssm_im[0])], axis=-1)
    zs, hs = _s5_step(us, h0, sw, ssm_d)
    s_ssm_re = hs[:, :, :S5_STATE].transpose(1, 0, 2).reshape(1, bs, SSM_G, SSM_P)
    s_ssm_im = hs[:, :, S5_STATE:].transpose(1, 0, 2).reshape(1, bs, SSM_G, SSM_P)

    def mem_kv_bf(x, b):
        return x.reshape(b, n_mem, MEM_W).astype(BF16)

    memp = _mem_attn(qmp.reshape(bp, tp, MEM_W), mem_kv_bf(p_mem_k[0], bp), mem_kv_bf(p_mem_v[0], bp), tm)
    mems = _mem_attn(qms.reshape(bs, 1, MEM_W), mem_kv_bf(cache_mem_k[0], bs), mem_kv_bf(cache_mem_v[0], bs), 1)

    x1p, rp = _mix(xp, zp.reshape(n_p, MAIN_W), memp.reshape(n_p, MEM_W), w_out_bf[0], ln1_g[0:1], ln1_b[0:1],
                   wr_bf[0], br[0], tm, w_glu_bf[0], b_glu[0:1])
    x1s, rs = _mix(xs, zs, mems.reshape(bs, MEM_W), w_out_bf[0], ln1_g[0:1], ln1_b[0:1],
                   wr_bf[0], br[0], bs, w_glu_bf[0], b_glu[0:1])
    x2p, x2s = _moe_layer([x1p, x1s], [rp, rs], wgu_bf[0], wd_bf[0], ln2_g[0:1], ln2_b[0:1])

    kp, vp, kpb, vpb, lfp, cp = _kvf(x2p, wkv_bf, wf_bf, bf_pad, bp, tm)
    ks, vs, _, _, lfs, _ = _kvf(x2s, wkv_bf, wf_bf, bf_pad, 1, bs)
    p_k = kp.reshape(bp, tp, FOX_H, FOX_DH)
    p_v = vp.reshape(bp, tp, FOX_H, FOX_DH)
    p_logf = lfp[:, :FOX_H].reshape(bp, tp, FOX_H)
    s_k = ks.reshape(bs, 1, FOX_H, FOX_DH)
    s_v = vs.reshape(bs, 1, FOX_H, FOX_DH)
    s_logf = lfs[:, :FOX_H].reshape(bs, 1, FOX_H)

    qp, qmp = _proj(x2p, w_in_bf[1], (MAIN_W, MEM_W), (BF16, BF16), tm)
    qs, qms = _proj(x2s, w_in_bf[1], (MAIN_W, MEM_W), (F32, BF16), bs)
    scale = FOX_DH ** -0.5
    c_hm = cp[:, :FOX_H].reshape(bp, tp, FOX_H).transpose(0, 2, 1)
    attn = _fox_prompt(_heads(qp * jnp.asarray(scale, BF16), bp, tp), _heads(kpb, bp, tp), _heads(vpb, bp, tp),
                       c_hm[:, :, None, :], 2048)
    attn_p = attn.transpose(0, 2, 1, 3).reshape(n_p, MAIN_W)
    hpad = DEC_HEADS_PAD - FOX_H

    def pad_heads(x):
        return jnp.pad(x.reshape(bs, FOX_H, FOX_DH), ((0, 0), (0, hpad), (0, 0)))

    cache_lf_t = jnp.pad(cache_logf.transpose(0, 2, 1), ((0, 0), (0, hpad), (0, 0)))
    q_dec = pad_heads(qs * scale)
    attn_s = _fox_decode(page_table, q_dec, q_dec[..., None], pad_heads(ks), pad_heads(vs)[..., None],
                         lfs[:, :DEC_HEADS_PAD, None] * (jnp.arange(DEC_HEADS_PAD) < FOX_H)[None, :, None],
                         cache_k.transpose(0, 2, 3, 1), cache_v.transpose(0, 2, 3, 1),
                         cache_lf_t)[:, :FOX_H, :, 0].reshape(bs, MAIN_W)

    memp = _mem_attn(qmp.reshape(bp, tp, MEM_W), mem_kv_bf(p_mem_k[1], bp), mem_kv_bf(p_mem_v[1], bp), tm)
    mems = _mem_attn(qms.reshape(bs, 1, MEM_W), mem_kv_bf(cache_mem_k[1], bs), mem_kv_bf(cache_mem_v[1], bs), 1)
    x1p, rp = _mix(x2p, attn_p, memp.reshape(n_p, MEM_W), w_out_bf[1], ln1_g[1:2], ln1_b[1:2], wr_bf[1], br[1], tm)
    x1s, rs = _mix(x2s, attn_s, mems.reshape(bs, MEM_W), w_out_bf[1], ln1_g[1:2], ln1_b[1:2], wr_bf[1], br[1], bs)
    yp, ys = _moe_layer([x1p, x1s], [rp, rs], wgu_bf[1], wd_bf[1], ln2_g[1:2], ln2_b[1:2])

    return (yp.reshape(bp, tp, D_MODEL), ys.reshape(bs, 1, D_MODEL), p_ssm_re, p_ssm_im, p_k, p_v, p_logf,
            p_mem_k, p_mem_v, s_ssm_re, s_ssm_im, s_k, s_v, s_logf)
```
